```python
import math
import jax, jax.numpy as jnp
from jax import lax
import numpy as np

D_MODEL = 1024
BATCH = 2
SEQ = 8192
DEPTH = 2
DEC_BATCH = 32
DEC_SEQ = 4
PAST_LEN = 16384
PAGE_SIZE = 128

H_A = 4
D_HALF = 64
HEAD_DIM_A = 2 * D_HALF
ATTN_W = H_A * HEAD_DIM_A
SCALE = D_HALF ** -0.5
G_B = 4
CHUNK = 128
GC = 128
GMLP_W = G_B * GC
_FF_RAW = -(-8 * D_MODEL // 3)
D_FF = -(-_FF_RAW // 256) * 256
NUM_BUCKETS = 32
MAX_EXACT = 16
MAX_DISTANCE = 128
Q_BLOCK = 128
EPS = 1e-6
NEG_INF = -1e30
IN_W = 3 * ATTN_W + 2 * GMLP_W + 2 * D_MODEL
SPLITS = (ATTN_W, 2 * ATTN_W, 3 * ATTN_W, 3 * ATTN_W + GMLP_W,
          3 * ATTN_W + 2 * GMLP_W, 3 * ATTN_W + 2 * GMLP_W + D_MODEL)

kernel_name = "hybrid_diffattn_chunkgmlp_decode_step"


def rmsnorm(x, g):
    xf = x.astype(jnp.float32)
    y = xf * lax.rsqrt(jnp.mean(xf * xf, axis=-1, keepdims=True) + EPS) * g.astype(jnp.float32)
    return y.astype(x.dtype)


def t5_bucket(dist):
    n = jnp.maximum(dist, 0)
    nf = jnp.maximum(n, 1).astype(jnp.float32)
    large = MAX_EXACT + (jnp.log(nf / MAX_EXACT) / math.log(MAX_DISTANCE / MAX_EXACT)
                         * (NUM_BUCKETS - MAX_EXACT)).astype(jnp.int32)
    return jnp.where(n < MAX_EXACT, n, jnp.minimum(large, NUM_BUCKETS - 1))


def diff_logits(q, k, q_pos, k_pos, rel_bias):
    s = jnp.einsum('bqhmd,bkhmd->bmhqk', q, k, preferred_element_type=jnp.float32) * SCALE
    dist = q_pos[:, None] - k_pos[None, :]
    bias = jnp.moveaxis(rel_bias[t5_bucket(dist)], -1, 0).astype(jnp.float32)
    s = s + bias[None, None]
    return jnp.where((dist >= 0)[None, None, None], s, NEG_INF)


def diff_probs(s, lam):
    p = jax.nn.softmax(s, axis=-1)
    return p[:, 0] - lam * p[:, 1]


def weigh_values(p, v):
    return jnp.einsum('bhqk,bkhd->bqhd', p.astype(v.dtype), v, preferred_element_type=jnp.float32)


def attend_prompt(q, k, v, lam, rel_bias):
    nb_, s_len = q.shape[0], q.shape[1]
    n_blk = s_len // Q_BLOCK
    qb = jnp.moveaxis(q.reshape(nb_, n_blk, Q_BLOCK, H_A, 2, D_HALF), 1, 0)
    k_pos = jnp.arange(s_len, dtype=jnp.int32)

    def one_block(args):
        q_blk, i = args
        q_pos = i * Q_BLOCK + jnp.arange(Q_BLOCK, dtype=jnp.int32)
        p = diff_probs(diff_logits(q_blk, k, q_pos, k_pos, rel_bias), lam)
        return weigh_values(p, v)

    o = lax.map(one_block, (qb, jnp.arange(n_blk, dtype=jnp.int32)))
    return jnp.moveaxis(o, 0, 1).reshape(nb_, s_len, H_A, HEAD_DIM_A)


def attend_sample(q, k, v, lam, k_past, v_past, rel_bias):
    t_len = q.shape[1]
    p_len = k_past.shape[1]
    q_pos = p_len + jnp.arange(t_len, dtype=jnp.int32)
    s_past = diff_logits(q, k_past, q_pos, jnp.arange(p_len, dtype=jnp.int32), rel_bias)
    s_new = diff_logits(q, k, q_pos, q_pos, rel_bias)
    p = diff_probs(jnp.concatenate([s_past, s_new], axis=-1), lam)
    return weigh_values(p[..., :p_len], v_past) + weigh_values(p[..., p_len:], v)


def spatial_prompt(gvn, w_s, b_s):
    nb_, s_len = gvn.shape[0], gvn.shape[1]
    vc = gvn.reshape(nb_, s_len // CHUNK, CHUNK, G_B, GC)
    w = w_s * jnp.tril(jnp.ones((CHUNK, CHUNK), w_s.dtype))[None]
    s = jnp.einsum('gts,bnsgc->bntgc', w.astype(gvn.dtype), vc,
                   preferred_element_type=jnp.float32) + b_s.T.astype(jnp.float32)[:, :, None]
    return s.reshape(nb_, s_len, GMLP_W).astype(gvn.dtype)


def spatial_sample(gvn, w_s, b_s):
    t_len = gvn.shape[1]
    vc = gvn.reshape(gvn.shape[0], t_len, G_B, GC)
    w = (w_s * jnp.tril(jnp.ones((CHUNK, CHUNK), w_s.dtype))[None])[:, :t_len, :t_len]
    s = jnp.einsum('gts,bsgc->btgc', w.astype(gvn.dtype), vc,
                   preferred_element_type=jnp.float32) + b_s[:, :t_len].T.astype(jnp.float32)[:, :, None]
    return s.reshape(gvn.shape[0], t_len, GMLP_W).astype(gvn.dtype)


def trunk_layer(x, attend, spatial, lam, lam_init, g_mix_pre, w_in, g_subln, g_gv, w_s, b_s,
                w_a, w_b, w_out, g_mix_post, g_ffn_pre, w_ffn_in, w_ffn_out, g_ffn_post):
    lead = x.shape[:-1]
    h = rmsnorm(x, g_mix_pre)
    q, k, v, u, gv, ga, gb = jnp.split(h @ w_in, SPLITS, axis=-1)
    q = q.reshape(*lead, H_A, 2, D_HALF)
    k = k.reshape(*lead, H_A, 2, D_HALF)
    v = v.reshape(*lead, H_A, HEAD_DIM_A)
    a = attend(q, k, v, lam)
    a = (rmsnorm(a, g_subln) * (1.0 - lam_init)).reshape(*lead, ATTN_W).astype(x.dtype)
    gvn = rmsnorm(gv, g_gv)
    b = u * spatial(gvn, w_s, b_s)
    m = jax.nn.sigmoid(ga) * (a @ w_a) + jax.nn.sigmoid(gb) * (b @ w_b)
    x = x + rmsnorm(m @ w_out, g_mix_post)
    gate, up = jnp.split(rmsnorm(x, g_ffn_pre) @ w_ffn_in, 2, axis=-1)
    x = x + rmsnorm((jax.nn.silu(gate) * up) @ w_ffn_out, g_ffn_post)
    return x, k.reshape(*lead, H_A, HEAD_DIM_A), v, gvn


def setup_inputs(seed: int = 0) -> dict:
    key = jax.random.key(seed)
    ks = iter(jax.random.split(key, 32))

    def nrm(shape, scale):
        return scale * jax.random.normal(next(ks), shape, jnp.float32)

    n_pages = PAST_LEN // PAGE_SIZE
    n_used = DEC_BATCH * n_pages
    n_pool = n_used + n_used // 4
    perm = jax.random.permutation(next(ks), n_pool)[:n_used]
    page_table = perm.astype(jnp.int32).reshape(DEC_BATCH, n_pages)
    return {
        "x_prompt": nrm((BATCH, SEQ, D_MODEL), 1.0),
        "x_sample": nrm((DEC_BATCH, DEC_SEQ, D_MODEL), 1.0),
        "cache_k": nrm((DEPTH, n_pool, PAGE_SIZE, H_A, HEAD_DIM_A), 1.0),
        "cache_v": nrm((DEPTH, n_pool, PAGE_SIZE, H_A, HEAD_DIM_A), 1.0),
        "page_table": page_table,
        "rel_bias": nrm((NUM_BUCKETS, H_A), 0.5),
        "g_mix_pre": 1.0 + nrm((DEPTH, D_MODEL), 0.05),
        "w_in": nrm((DEPTH, D_MODEL, IN_W), D_MODEL ** -0.5),
        "lam_q1": nrm((DEPTH, D_HALF), 0.1),
        "lam_k1": nrm((DEPTH, D_HALF), 0.1),
        "lam_q2": nrm((DEPTH, D_HALF), 0.1),
        "lam_k2": nrm((DEPTH, D_HALF), 0.1),
        "g_subln": 1.0 + nrm((DEPTH, HEAD_DIM_A), 0.05),
        "g_gmlp_v": 1.0 + nrm((DEPTH, GMLP_W), 0.05),
        "w_spatial": nrm((DEPTH, G_B, CHUNK, CHUNK), CHUNK ** -0.5),
        "b_spatial": 1.0 + nrm((DEPTH, G_B, CHUNK), 0.1),
        "w_branch_a": nrm((DEPTH, ATTN_W, D_MODEL), ATTN_W ** -0.5),
        "w_branch_b": nrm((DEPTH, GMLP_W, D_MODEL), GMLP_W ** -0.5),
        "w_out": nrm((DEPTH, D_MODEL, D_MODEL), D_MODEL ** -0.5),
        "g_mix_post": 1.0 + nrm((DEPTH, D_MODEL), 0.05),
        "g_ffn_pre": 1.0 + nrm((DEPTH, D_MODEL), 0.05),
        "w_ffn_in": nrm((DEPTH, D_MODEL, 2 * D_FF), D_MODEL ** -0.5),
        "w_ffn_out": nrm((DEPTH, D_FF, D_MODEL), D_FF ** -0.5),
        "g_ffn_post": 1.0 + nrm((DEPTH, D_MODEL), 0.05),
    }


def reference(x_prompt, x_sample, cache_k, cache_v, page_table, rel_bias, g_mix_pre, w_in,
              lam_q1, lam_k1, lam_q2, lam_k2, g_subln, g_gmlp_v, w_spatial, b_spatial,
              w_branch_a, w_branch_b, w_out, g_mix_post, g_ffn_pre, w_ffn_in, w_ffn_out,
              g_ffn_post):
    n_seq = page_table.shape[0]
    past_len = page_table.shape[1] * cache_k.shape[2]
    xp, xs = x_prompt, x_sample
    kp_l, vp_l, ks_l, vs_l, gvs_l = [], [], [], [], []
    for l in range(DEPTH):
        lam_init = 0.8 - 0.6 * math.exp(-0.3 * l)
        f32 = jnp.float32
        lam = (jnp.exp(jnp.sum(lam_q1[l].astype(f32) * lam_k1[l].astype(f32)))
               - jnp.exp(jnp.sum(lam_q2[l].astype(f32) * lam_k2[l].astype(f32))) + lam_init)
        weights = (g_mix_pre[l], w_in[l], g_subln[l], g_gmlp_v[l], w_spatial[l], b_spatial[l],
                   w_branch_a[l], w_branch_b[l], w_out[l], g_mix_post[l], g_ffn_pre[l],
                   w_ffn_in[l], w_ffn_out[l], g_ffn_post[l])

        def attend_p(q, k, v, lam_):
            return attend_prompt(q, k, v, lam_, rel_bias)

        k_past = cache_k[l, page_table].reshape(n_seq, past_len, H_A, 2, D_HALF)
        v_past = cache_v[l, page_table].reshape(n_seq, past_len, H_A, HEAD_DIM_A)

        def attend_s(q, k, v, lam_, k_past=k_past, v_past=v_past):
            return attend_sample(q, k, v, lam_, k_past, v_past, rel_bias)

        xp, kp, vp, _ = trunk_layer(xp, attend_p, spatial_prompt, lam, lam_init, *weights)
        xs, kn, vn, gvn = trunk_layer(xs, attend_s, spatial_sample, lam, lam_init, *weights)
        kp_l.append(kp)
        vp_l.append(vp)
        ks_l.append(kn)
        vs_l.append(vn)
        gvs_l.append(gvn)
    k_prompt = jnp.stack(kp_l)
    v_prompt = jnp.stack(vp_l)
    k_sample = jnp.stack(ks_l)
    v_sample = jnp.stack(vs_l)
    gv_sample = jnp.stack(gvs_l)
    return (xp, xs, k_prompt, v_prompt, k_sample, v_sample, gv_sample)
```

```python
import functools
import math

import jax
import jax.numpy as jnp
from jax import lax
from jax.experimental import pallas as pl
from jax.experimental.pallas import tpu as pltpu

F32 = jnp.float32
BF16 = jnp.bfloat16

LANES = 128
SUBLANES = 8
VMEM_LIMIT_BYTES = 56 * 1024 * 1024

N_HEADS = 4
D_HALF = 64
HEAD_DIM = 2 * D_HALF
ATTN_W = N_HEADS * HEAD_DIM
N_GROUPS = 4
CHUNK = 128
GMLP_W = N_GROUPS * CHUNK
SCALE = D_HALF ** -0.5
MAX_EXACT = 16
MAX_DISTANCE = 128
EPS = 1e-6
NEG_INF = -1e30

ATTN_TILE = 512
PAGES_PER_STEP = 8
ROW_TILE = 512
FF_TILE = 256

_NT = (((1,), (1,)), ((), ()))


def _rms(x, g):
    return x * lax.rsqrt(jnp.mean(x * x, axis=-1, keepdims=True) + EPS) * g


def _t5_bucket(dist, num_buckets):
    n = jnp.maximum(dist, 0)
    nf = jnp.maximum(n, 1).astype(F32)
    large = MAX_EXACT + (jnp.log(nf / MAX_EXACT) / math.log(MAX_DISTANCE / MAX_EXACT)
                         * (num_buckets - MAX_EXACT)).astype(jnp.int32)
    return jnp.where(n < MAX_EXACT, n, jnp.minimum(large, num_buckets - 1))


def _resident(shape):
    return pl.BlockSpec(shape, lambda *_: (0,) * len(shape), pipeline_mode=pl.Buffered(1))


def _proj_kernel(x_ref, g_ref, w_ref, ggv_ref, qkv_ref, k_ref, v_ref, u_ref, gvn_ref, ga_ref, gb_ref):
    d_model = x_ref.shape[1]
    h = _rms(x_ref[...], g_ref[...]).astype(BF16)

    def seg(start, width):
        return jnp.dot(h, w_ref[:, start:start + width], preferred_element_type=F32)

    q = seg(0, ATTN_W)
    k = seg(ATTN_W, ATTN_W)
    v = seg(2 * ATTN_W, ATTN_W)
    qkv_ref[:, 0:ATTN_W] = (q * SCALE).astype(BF16)
    qkv_ref[:, ATTN_W:2 * ATTN_W] = k.astype(BF16)
    qkv_ref[:, 2 * ATTN_W:3 * ATTN_W] = v.astype(BF16)
    k_ref[...] = k
    v_ref[...] = v
    u_ref[...] = seg(3 * ATTN_W, GMLP_W)
    gvn_ref[...] = _rms(seg(3 * ATTN_W + GMLP_W, GMLP_W), ggv_ref[...])
    ga_ref[...] = seg(3 * ATTN_W + 2 * GMLP_W, d_model)
    gb_ref[...] = seg(3 * ATTN_W + 2 * GMLP_W + d_model, d_model)


def _proj(x, g_pre, w_in, g_gv, tm):
    n, d = x.shape
    in_w = w_in.shape[1]
    row = lambda w: pl.BlockSpec((tm, w), lambda i: (i, 0))
    widths = (3 * ATTN_W, ATTN_W, ATTN_W, GMLP_W, GMLP_W, d, d)
    dtypes = (BF16, F32, F32, F32, F32, F32, F32)
    return pl.pallas_call(
        _proj_kernel,
        grid=(n // tm,),
        in_specs=[row(d), _resident((1, d)), _resident((d, in_w)), _resident((1, GMLP_W))],
        out_specs=[row(w) for w in widths],
        out_shape=[jax.ShapeDtypeStruct((n, w), t) for w, t in zip(widths, dtypes)],
        compiler_params=pltpu.CompilerParams(dimension_semantics=("arbitrary",),
                                             vmem_limit_bytes=VMEM_LIMIT_BYTES),
    )(x, g_pre, w_in, g_gv)


def _bias_tile_kernel(rb_ref, o_ref, *, tile, num_buckets):
    h = pl.program_id(0)
    off = pl.program_id(1)
    r = lax.broadcasted_iota(jnp.int32, (tile, tile), 0)
    c = lax.broadcasted_iota(jnp.int32, (tile, tile), 1)
    dist = off * tile + r - c
    bucket = _t5_bucket(dist, num_buckets)
    bias = jnp.zeros((tile, tile), F32)
    for j in range(num_buckets):
        bias = jnp.where(bucket == j, rb_ref[j, h], bias)
    o_ref[...] = jnp.where(dist >= 0, bias, NEG_INF)


def _bias_tiles(rel_bias, tile):
    nb, nh = rel_bias.shape
    return pl.pallas_call(
        functools.partial(_bias_tile_kernel, tile=tile, num_buckets=nb),
        grid=(nh, 2),
        in_specs=[pl.BlockSpec(memory_space=pltpu.SMEM)],
        out_specs=pl.BlockSpec((None, None, tile, tile), lambda h, o: (h, o, 0, 0)),
        out_shape=jax.ShapeDtypeStruct((nh, 2, tile, tile), F32),
        compiler_params=pltpu.CompilerParams(dimension_semantics=("arbitrary", "arbitrary"),
                                             vmem_limit_bytes=VMEM_LIMIT_BYTES),
    )(rel_bias)


def _lam(lam_ref, lam_init):
    lp = lam_ref[...]
    s1 = jnp.sum(lp[0:1] * lp[1:2], axis=-1, keepdims=True)
    s2 = jnp.sum(lp[2:3] * lp[3:4], axis=-1, keepdims=True)
    return jnp.exp(s1) - jnp.exp(s2) + lam_init


def _attn_kernel(q_ref, k_ref, v_ref, bias_ref, rb_ref, lam_ref, gsub_ref, o_ref,
                 qz_ref, m_ref, l_ref, acc_ref, *, tile, lam_init, far_bucket):
    h = pl.program_id(1)
    i = pl.program_id(2)

    q = q_ref[...]
    lane = lax.broadcasted_iota(jnp.int32, q.shape, 1)
    qz_ref[0] = jnp.where(lane < D_HALF, q, jnp.zeros_like(q))
    qz_ref[1] = jnp.where(lane >= D_HALF, q, jnp.zeros_like(q))
    m_ref[...] = jnp.full(m_ref.shape, NEG_INF, F32)
    l_ref[...] = jnp.zeros(l_ref.shape, F32)
    acc_ref[...] = jnp.zeros(acc_ref.shape, F32)

    def step(j, bias):
        off = pl.multiple_of(j * tile, tile)
        kb = k_ref[pl.ds(off, tile), :]
        vb = v_ref[pl.ds(off, tile), :]
        for mp in range(2):
            s = lax.dot_general(qz_ref[mp], kb, _NT, preferred_element_type=F32) + bias
            m_prev = m_ref[mp]
            m_new = jnp.maximum(m_prev, jnp.max(s, axis=-1, keepdims=True))
            alpha = jnp.exp(m_prev - m_new)
            p = jnp.exp(s - m_new)
            l_ref[mp] = alpha * l_ref[mp] + jnp.sum(p, axis=-1, keepdims=True)
            acc_ref[mp] = alpha * acc_ref[mp] + jnp.dot(p.astype(BF16), vb, preferred_element_type=F32)
            m_ref[mp] = m_new

    far_bias = rb_ref[far_bucket, h]

    def far_body(j, carry):
        step(j, far_bias)
        return carry

    lax.fori_loop(0, jnp.maximum(i - 1, 0), far_body, 0)

    @pl.when(i >= 1)
    def _():
        step(i - 1, bias_ref[1])

    step(i, bias_ref[0])

    lam = _lam(lam_ref, lam_init)
    o = acc_ref[0] / l_ref[0] - lam * (acc_ref[1] / l_ref[1])
    o_ref[...] = (_rms(o, gsub_ref[...]) * (1.0 - lam_init)).astype(o_ref.dtype)


def _attn_prompt(qkv, bias_tiles, rel_bias, lam_params, g_subln, batch, seq, lam_init):
    tile = bias_tiles.shape[-1]
    assert tile >= MAX_DISTANCE and seq % tile == 0
    nq = seq // tile
    nb = rel_bias.shape[0]
    kernel = functools.partial(_attn_kernel, tile=tile, lam_init=lam_init, far_bucket=nb - 1)
    return pl.pallas_call(
        kernel,
        grid=(batch, N_HEADS, nq),
        in_specs=[
            pl.BlockSpec((tile, HEAD_DIM), lambda b, h, i: (b * nq + i, h)),
            pl.BlockSpec((seq, HEAD_DIM), lambda b, h, i: (b, N_HEADS + h)),
            pl.BlockSpec((seq, HEAD_DIM), lambda b, h, i: (b, 2 * N_HEADS + h)),
            pl.BlockSpec((None, 2, tile, tile), lambda b, h, i: (h, 0, 0, 0)),
            pl.BlockSpec(memory_space=pltpu.SMEM),
            pl.BlockSpec((4, D_HALF), lambda b, h, i: (0, 0)),
            pl.BlockSpec((1, HEAD_DIM), lambda b, h, i: (0, 0)),
        ],
        out_specs=pl.BlockSpec((tile, HEAD_DIM), lambda b, h, i: (b * nq + i, h)),
        out_shape=jax.ShapeDtypeStruct((batch * seq, ATTN_W), BF16),
        scratch_shapes=[
            pltpu.VMEM((2, tile, HEAD_DIM), BF16),
            pltpu.VMEM((2, tile, 1), F32),
            pltpu.VMEM((2, tile, 1), F32),
            pltpu.VMEM((2, tile, HEAD_DIM), F32),
        ],
        compiler_params=pltpu.CompilerParams(
            dimension_semantics=("arbitrary", "arbitrary", "arbitrary"),
            vmem_limit_bytes=VMEM_LIMIT_BYTES),
    )(qkv, qkv, qkv, bias_tiles, rel_bias, lam_params, g_subln)


def _row_bias(dist, hrow, rb_ref, num_buckets):
    bucket = _t5_bucket(dist, num_buckets)
    out = jnp.zeros(dist.shape, F32)
    for j in range(num_buckets):
        col = jnp.full(hrow.shape, rb_ref[j, 0], F32)
        for hh in range(1, N_HEADS):
            col = jnp.where(hrow == hh, rb_ref[j, hh], col)
        out = jnp.where(bucket == j, col, out)
    return out


def _sattn_kernel(pt_ref, q_ref, kn_ref, vn_ref, lam_ref, gsub_ref, rb_ref, *rest,
                  n_pages, page, t_new, lam_init, num_buckets):
    del pt_ref
    gp = PAGES_PER_STEP
    k_refs, v_refs = rest[:gp], rest[gp:2 * gp]
    o_ref, qbd_ref, kbuf, vbuf, bias_ref, m_ref, l_ref, acc_ref = rest[2 * gp:]
    g = pl.program_id(1)
    n_steps = pl.num_programs(1)
    past = n_pages * page
    rows = 2 * N_HEADS * SUBLANES
    blk = gp * page

    r = lax.broadcasted_iota(jnp.int32, (rows, 1), 0)
    hrow = (r // SUBLANES) % N_HEADS
    trow = r % SUBLANES

    @pl.when(g == 0)
    def _():
        q8 = q_ref[...]
        lane = lax.broadcasted_iota(jnp.int32, q8.shape, 1)
        pieces = []
        for mp in range(2):
            for hh in range(N_HEADS):
                lo = hh * HEAD_DIM + mp * D_HALF
                pieces.append(jnp.where((lane >= lo) & (lane < lo + D_HALF), q8, 0.0))
        qbd_ref[...] = jnp.concatenate(pieces, axis=0)
        m_ref[...] = jnp.full(m_ref.shape, NEG_INF, F32)
        l_ref[...] = jnp.zeros(l_ref.shape, F32)
        acc_ref[...] = jnp.zeros(acc_ref.shape, F32)

    for j in range(gp):
        kbuf[j * page:(j + 1) * page, :] = k_refs[j][...].astype(BF16)
        vbuf[j * page:(j + 1) * page, :] = v_refs[j][...].astype(BF16)

    near = past - (g + 1) * blk + 1 < MAX_DISTANCE

    @pl.when(near)
    def _():
        kpos = g * blk + lax.broadcasted_iota(jnp.int32, (rows, blk), 1)
        dist = past + trow - kpos
        bias_ref[...] = jnp.where(dist >= 0, _row_bias(dist, hrow, rb_ref, num_buckets), NEG_INF)

    @pl.when(jnp.logical_not(near))
    def _():
        col = jnp.full(hrow.shape, rb_ref[num_buckets - 1, 0], F32)
        for hh in range(1, N_HEADS):
            col = jnp.where(hrow == hh, rb_ref[num_buckets - 1, hh], col)
        bias_ref[...] = jnp.broadcast_to(col, bias_ref.shape)

    s = lax.dot_general(qbd_ref[...].astype(BF16), kbuf[...], _NT, preferred_element_type=F32)
    s = s + bias_ref[...]
    m_prev = m_ref[...]
    m_new = jnp.maximum(m_prev, jnp.max(s, axis=-1, keepdims=True))
    alpha = jnp.exp(m_prev - m_new)
    p = jnp.exp(s - m_new)
    l_ref[...] = alpha * l_ref[...] + jnp.sum(p, axis=-1, keepdims=True)
    acc_ref[...] = alpha * acc_ref[...] + jnp.dot(p.astype(BF16), vbuf[...], preferred_element_type=F32)
    m_ref[...] = m_new

    @pl.when(g == n_steps - 1)
    def _():
        qbd = qbd_ref[...]
        lane = lax.broadcasted_iota(jnp.int32, (rows, LANES), 1)
        s_new = jnp.zeros((rows, LANES), F32)
        for t in range(t_new):
            col = jnp.sum(qbd * kn_ref[t:t + 1, :], axis=-1, keepdims=True)
            s_new = jnp.where(lane == t, col, s_new)
        dist = trow - lane
        valid = (dist >= 0) & (lane < t_new)
        s_new = jnp.where(valid, s_new + _row_bias(dist, hrow, rb_ref, num_buckets), NEG_INF)
        m_prev = m_ref[...]
        m_fin = jnp.maximum(m_prev, jnp.max(s_new, axis=-1, keepdims=True))
        alpha = jnp.exp(m_prev - m_fin)
        p_new = jnp.exp(s_new - m_fin)
        l_fin = alpha * l_ref[...] + jnp.sum(p_new, axis=-1, keepdims=True)
        acc = alpha * acc_ref[...]
        for t in range(t_new):
            pt = jnp.sum(jnp.where(lane == t, p_new, 0.0), axis=-1, keepdims=True)
            acc = acc + pt * vn_ref[t:t + 1, :]
        o = acc / l_fin
        lam = _lam(lam_ref, lam_init)
        half = rows // 2
        diff = o[:half] - lam * o[half:]
        outs = []
        for hh in range(N_HEADS):
            oh = diff[hh * SUBLANES:(hh + 1) * SUBLANES, hh * HEAD_DIM:(hh + 1) * HEAD_DIM]
            outs.append(_rms(oh, gsub_ref[...]) * (1.0 - lam_init))
        o_ref[...] = jnp.concatenate(outs, axis=1)


def _attn_sample(q8, kn8, vn8, cache_k, cache_v, page_table, layer, rel_bias, lam_params, g_subln,
                 t_new, lam_init):
    nseq = q8.shape[0]
    depth, n_pool, page = cache_k.shape[:3]
    n_pages = page_table.shape[1]
    gp = PAGES_PER_STEP
    assert n_pages % gp == 0 and t_new <= SUBLANES
    ck = cache_k.reshape(depth * n_pool, page, ATTN_W)
    cv = cache_v.reshape(depth * n_pool, page, ATTN_W)
    base = layer * n_pool
    rows = 2 * N_HEADS * SUBLANES
    nb = rel_bias.shape[0]

    def page_spec(j):
        return pl.BlockSpec((None, page, ATTN_W),
                            lambda b, g, pt: (base + pt[b * n_pages + g * gp + j], 0, 0))

    seq_spec = pl.BlockSpec((None, SUBLANES, ATTN_W), lambda b, g, pt: (b, 0, 0))
    kernel = functools.partial(_sattn_kernel, n_pages=n_pages, page=page, t_new=t_new,
                               lam_init=lam_init, num_buckets=nb)
    grid_spec = pltpu.PrefetchScalarGridSpec(
        num_scalar_prefetch=1,
        grid=(nseq, n_pages // gp),
        in_specs=[seq_spec, seq_spec, seq_spec,
                  pl.BlockSpec((4, D_HALF), lambda b, g, pt: (0, 0)),
                  pl.BlockSpec((1, HEAD_DIM), lambda b, g, pt: (0, 0)),
                  pl.BlockSpec(memory_space=pltpu.SMEM)]
                 + [page_spec(j) for j in range(gp)] + [page_spec(j) for j in range(gp)],
        out_specs=seq_spec,
        scratch_shapes=[
            pltpu.VMEM((rows, ATTN_W), F32),
            pltpu.VMEM((gp * page, ATTN_W), BF16),
            pltpu.VMEM((gp * page, ATTN_W), BF16),
            pltpu.VMEM((rows, gp * page), F32),
            pltpu.VMEM((rows, 1), F32),
            pltpu.VMEM((rows, 1), F32),
            pltpu.VMEM((rows, ATTN_W), F32),
        ],
    )
    return pl.pallas_call(
        kernel,
        grid_spec=grid_spec,
        out_shape=jax.ShapeDtypeStruct((nseq, SUBLANES, ATTN_W), F32),
        compiler_params=pltpu.CompilerParams(dimension_semantics=("arbitrary", "arbitrary"),
                                             vmem_limit_bytes=VMEM_LIMIT_BYTES),
    )(page_table.reshape(-1), q8, kn8, vn8, lam_params, g_subln, rel_bias,
      *([ck] * gp), *([cv] * gp))


def _mixer_kernel(a_ref, u_ref, gvn_ref, ga_ref, gb_ref, x_ref, ws_ref, bs_ref, wa_ref, wb_ref,
                  wo_ref, gpost_ref, o_ref, b_scr):
    tm = x_ref.shape[0]
    r = lax.broadcasted_iota(jnp.int32, (CHUNK, CHUNK), 0)
    c = lax.broadcasted_iota(jnp.int32, (CHUNK, CHUNK), 1)
    causal = r >= c
    for grp in range(N_GROUPS):
        cols = slice(grp * CHUNK, (grp + 1) * CHUNK)
        wg = ws_ref[grp]
        wg = jnp.where(causal, wg, jnp.zeros_like(wg))
        for ch in range(tm // CHUNK):
            rws = slice(ch * CHUNK, (ch + 1) * CHUNK)
            sp = jnp.dot(wg, gvn_ref[rws, cols].astype(BF16), preferred_element_type=F32)
            sp = sp + bs_ref[:, cols]
            b_scr[rws, cols] = (u_ref[rws, cols] * sp).astype(BF16)
    ya = jnp.dot(a_ref[...], wa_ref[...], preferred_element_type=F32)
    yb = jnp.dot(b_scr[...], wb_ref[...], preferred_element_type=F32)
    mix = jax.nn.sigmoid(ga_ref[...]) * ya + jax.nn.sigmoid(gb_ref[...]) * yb
    mo = jnp.dot(mix.astype(BF16), wo_ref[...], preferred_element_type=F32)
    o_ref[...] = x_ref[...] + _rms(mo, gpost_ref[...])


def _mixer(a, u, gvn, ga, gb, x, ws, bs_tile, w_a, w_b, w_out, g_post, tm):
    n, d = x.shape
    row = lambda w: pl.BlockSpec((tm, w), lambda i: (i, 0))
    return pl.pallas_call(
        _mixer_kernel,
        grid=(n // tm,),
        in_specs=[row(ATTN_W), row(GMLP_W), row(GMLP_W), row(d), row(d), row(d),
                  _resident(ws.shape), _resident(bs_tile.shape), _resident(w_a.shape),
                  _resident(w_b.shape), _resident(w_out.shape), _resident((1, d))],
        out_specs=row(d),
        out_shape=jax.ShapeDtypeStruct((n, d), F32),
        scratch_shapes=[pltpu.VMEM((tm, GMLP_W), BF16)],
        compiler_params=pltpu.CompilerParams(dimension_semantics=("arbitrary",),
                                             vmem_limit_bytes=VMEM_LIMIT_BYTES),
    )(a, u, gvn, ga, gb, x, ws, bs_tile, w_a, w_b, w_out, g_post)


def _ffn_kernel(x_ref, gpre_ref, w1_ref, w2_ref, gpost_ref, o_ref):
    d_ff = w2_ref.shape[0]
    x = x_ref[...]
    h = _rms(x, gpre_ref[...]).astype(BF16)
    acc = jnp.zeros(x.shape, F32)
    for c0 in range(0, d_ff, FF_TILE):
        gate = jnp.dot(h, w1_ref[:, c0:c0 + FF_TILE], preferred_element_type=F32)
        up = jnp.dot(h, w1_ref[:, d_ff + c0:d_ff + c0 + FF_TILE], preferred_element_type=F32)
        act = (gate * jax.nn.sigmoid(gate) * up).astype(BF16)
        acc = acc + jnp.dot(act, w2_ref[c0:c0 + FF_TILE, :], preferred_element_type=F32)
    o_ref[...] = x + _rms(acc, gpost_ref[...])


def _ffn(x, g_pre, w1, w2, g_post, tm):
    n, d = x.shape
    assert w2.shape[0] % FF_TILE == 0
    row = pl.BlockSpec((tm, d), lambda i: (i, 0))
    return pl.pallas_call(
        _ffn_kernel,
        grid=(n // tm,),
        in_specs=[row, _resident((1, d)), _resident(w1.shape), _resident(w2.shape), _resident((1, d))],
        out_specs=row,
        out_shape=jax.ShapeDtypeStruct((n, d), F32),
        compiler_params=pltpu.CompilerParams(dimension_semantics=("arbitrary",),
                                             vmem_limit_bytes=VMEM_LIMIT_BYTES),
    )(x, g_pre, w1, w2, g_post)


def kernel(x_prompt, x_sample, cache_k, cache_v, page_table, rel_bias, g_mix_pre, w_in, lam_q1, lam_k1, lam_q2, lam_k2, g_subln, g_gmlp_v, w_spatial, b_spatial, w_branch_a, w_branch_b, w_out, g_mix_post, g_ffn_pre, w_ffn_in, w_ffn_out, g_ffn_post):
    batch, seq, d = x_prompt.shape
    nseq, t_new, _ = x_sample.shape
    depth = w_in.shape[0]
    n_s = nseq * t_new
    assert n_s % CHUNK == 0 and CHUNK % t_new == 0
    tm_p = min(ROW_TILE, batch * seq)
    tm_s = CHUNK

    bias_tiles = _bias_tiles(rel_bias, min(ATTN_TILE, seq))
    xp = x_prompt.reshape(batch * seq, d)
    xs = x_sample.reshape(n_s, d)
    row = lambda v: v.reshape(1, -1)
    eye = jnp.eye(CHUNK // t_new, dtype=F32)

    outs = {name: [] for name in ("kp", "vp", "ks", "vs", "gvs")}
    for l in range(depth):
        lam_init = 0.8 - 0.6 * math.exp(-0.3 * l)
        lam_params = jnp.stack([lam_q1[l], lam_k1[l], lam_q2[l], lam_k2[l]])
        w_in_l = w_in[l].astype(BF16)
        w_a, w_b, w_o = (w_branch_a[l].astype(BF16), w_branch_b[l].astype(BF16), w_out[l].astype(BF16))
        w1, w2 = w_ffn_in[l].astype(BF16), w_ffn_out[l].astype(BF16)
        gsub = row(g_subln[l])
        ws_p = w_spatial[l].astype(BF16)
        bs_p = jnp.repeat(b_spatial[l].T, CHUNK, axis=1)
        ws_s = jax.vmap(lambda w: jnp.kron(eye, w[:t_new, :t_new]))(w_spatial[l]).astype(BF16)
        bs_s = jnp.repeat(jnp.tile(b_spatial[l][:, :t_new], (1, CHUNK // t_new)).T, CHUNK, axis=1)

        qkv, kp, vp, u, gvn, ga, gb = _proj(xp, row(g_mix_pre[l]), w_in_l, row(g_gmlp_v[l]), tm_p)
        a = _attn_prompt(qkv, bias_tiles, rel_bias, lam_params, gsub, batch, seq, lam_init)
        xp = _mixer(a, u, gvn, ga, gb, xp, ws_p, bs_p, w_a, w_b, w_o, row(g_mix_post[l]), tm_p)
        xp = _ffn(xp, row(g_ffn_pre[l]), w1, w2, row(g_ffn_post[l]), tm_p)
        outs["kp"].append(kp.reshape(batch, seq, N_HEADS, HEAD_DIM))
        outs["vp"].append(vp.reshape(batch, seq, N_HEADS, HEAD_DIM))

        qkv, ks, vs, u, gvn, ga, gb = _proj(xs, row(g_mix_pre[l]), w_in_l, row(g_gmlp_v[l]), tm_s)
        pad = lambda t: jnp.pad(t.astype(F32).reshape(nseq, t_new, ATTN_W),
                                ((0, 0), (0, SUBLANES - t_new), (0, 0)))
        a8 = _attn_sample(pad(qkv[:, :ATTN_W]), pad(ks), pad(vs), cache_k, cache_v, page_table, l,
                          rel_bias, lam_params, gsub, t_new, lam_init)
        a = a8[:, :t_new].reshape(n_s, ATTN_W).astype(BF16)
        xs = _mixer(a, u, gvn, ga, gb, xs, ws_s, bs_s, w_a, w_b, w_o, row(g_mix_post[l]), tm_s)
        xs = _ffn(xs, row(g_ffn_pre[l]), w1, w2, row(g_ffn_post[l]), tm_s)
        outs["ks"].append(ks.reshape(nseq, t_new, N_HEADS, HEAD_DIM))
        outs["vs"].append(vs.reshape(nseq, t_new, N_HEADS, HEAD_DIM))
        outs["gvs"].append(gvn.reshape(nseq, t_new, GMLP_W))

    return (xp.reshape(batch, seq, d), xs.reshape(nseq, t_new, d),
            jnp.stack(outs["kp"]), jnp.stack(outs["vp"]),
            jnp.stack(outs["ks"]), jnp.stack(outs["vs"]), jnp.stack(outs["gvs"]))
```

```python
import functools
import math

import jax
import jax.numpy as jnp
from jax import lax
from jax.experimental import pallas as pl
from jax.experimental.pallas import tpu as pltpu

F32 = jnp.float32
BF16 = jnp.bfloat16

LANES = 128
SUBLANES = 8
VMEM_LIMIT_BYTES = 56 * 1024 * 1024

N_HEADS = 4
D_HALF = 64
HEAD_DIM = 2 * D_HALF
ATTN_W = N_HEADS * HEAD_DIM
N_GROUPS = 4
CHUNK = 128
GMLP_W = N_GROUPS * CHUNK
SCALE = D_HALF ** -0.5
MAX_EXACT = 16
MAX_DISTANCE = 128
EPS = 1e-6
NEG_INF = -1e30

ATTN_TILE = 512
PAGES_PER_STEP = 8
ROW_TILE = 512
FF_TILE = 256

_NT = (((1,), (1,)), ((), ()))


def _rms(x, g):
    return x * lax.rsqrt(jnp.mean(x * x, axis=-1, keepdims=True) + EPS) * g


def _t5_bucket(dist, num_buckets):
    n = jnp.maximum(dist, 0)
    nf = jnp.maximum(n, 1).astype(F32)
    large = MAX_EXACT + (jnp.log(nf / MAX_EXACT) / math.log(MAX_DISTANCE / MAX_EXACT)
                         * (num_buckets - MAX_EXACT)).astype(jnp.int32)
    return jnp.where(n < MAX_EXACT, n, jnp.minimum(large, num_buckets - 1))


def _resident(shape):
    return pl.BlockSpec(shape, lambda *_: (0,) * len(shape), pipeline_mode=pl.Buffered(1))


def _params(n_axes):
    return pltpu.CompilerParams(dimension_semantics=("arbitrary",) * n_axes,
                                vmem_limit_bytes=VMEM_LIMIT_BYTES)


def _proj_kernel(x_ref, g_ref, w_ref, ggv_ref, qk_ref, vt_ref, k_ref, v_ref, u_ref, gvn_ref, ga_ref, gb_ref):
    d_model = x_ref.shape[1]
    h = _rms(x_ref[...], g_ref[...]).astype(BF16)

    def seg(start, width):
        return jnp.dot(h, w_ref[:, start:start + width], preferred_element_type=F32)

    q = seg(0, ATTN_W)
    k = seg(ATTN_W, ATTN_W)
    v = seg(2 * ATTN_W, ATTN_W)
    qk_ref[:, 0:ATTN_W] = (q * SCALE).astype(BF16)
    qk_ref[:, ATTN_W:2 * ATTN_W] = k.astype(BF16)
    vt_ref[...] = v.T.astype(BF16)
    k_ref[...] = k
    v_ref[...] = v
    u_ref[...] = seg(3 * ATTN_W, GMLP_W)
    gvn_ref[...] = _rms(seg(3 * ATTN_W + GMLP_W, GMLP_W), ggv_ref[...])
    ga_ref[...] = seg(3 * ATTN_W + 2 * GMLP_W, d_model)
    gb_ref[...] = seg(3 * ATTN_W + 2 * GMLP_W + d_model, d_model)


def _proj(x, g_pre, w_in, g_gv, tm):
    n, d = x.shape
    in_w = w_in.shape[1]
    row = lambda w: pl.BlockSpec((tm, w), lambda i: (i, 0))
    widths = (ATTN_W, ATTN_W, GMLP_W, GMLP_W, d, d)
    return pl.pallas_call(
        _proj_kernel,
        grid=(n // tm,),
        in_specs=[row(d), _resident((1, d)), _resident((d, in_w)), _resident((1, GMLP_W))],
        out_specs=[row(2 * ATTN_W), pl.BlockSpec((None, ATTN_W, tm), lambda i: (i, 0, 0))]
                  + [row(w) for w in widths],
        out_shape=[jax.ShapeDtypeStruct((n, 2 * ATTN_W), BF16),
                   jax.ShapeDtypeStruct((n // tm, ATTN_W, tm), BF16)]
                  + [jax.ShapeDtypeStruct((n, w), F32) for w in widths],
        compiler_params=_params(1),
        name="proj",
    )(x, g_pre, w_in, g_gv)


def _bias_tile_kernel(rb_ref, o_ref, *, tile, num_buckets):
    h = pl.program_id(0)
    off = pl.program_id(1)
    key = lax.broadcasted_iota(jnp.int32, (tile, tile), 0)
    qry = lax.broadcasted_iota(jnp.int32, (tile, tile), 1)
    dist = off * tile + qry - key
    bucket = _t5_bucket(dist, num_buckets)
    bias = jnp.zeros((tile, tile), F32)
    for j in range(num_buckets):
        bias = jnp.where(bucket == j, rb_ref[j, h], bias)
    o_ref[...] = jnp.where(dist >= 0, bias, NEG_INF)


def _bias_tiles(rel_bias, tile):
    nb, nh = rel_bias.shape
    return pl.pallas_call(
        functools.partial(_bias_tile_kernel, tile=tile, num_buckets=nb),
        grid=(nh, 2),
        in_specs=[pl.BlockSpec(memory_space=pltpu.SMEM)],
        out_specs=pl.BlockSpec((None, None, tile, tile), lambda h, o: (h, o, 0, 0)),
        out_shape=jax.ShapeDtypeStruct((nh, 2, tile, tile), F32),
        compiler_params=_params(2),
        name="bias_tiles",
    )(rel_bias)


def _lam(lam_ref, lam_init):
    lp = lam_ref[...]
    s1 = jnp.sum(lp[0:1] * lp[1:2], axis=-1, keepdims=True)
    s2 = jnp.sum(lp[2:3] * lp[3:4], axis=-1, keepdims=True)
    return jnp.exp(s1) - jnp.exp(s2) + lam_init


def _attn_kernel(q_ref, k_ref, vt_ref, bias_ref, rb_ref, lam_ref, gsub_ref, o_ref,
                 qz_ref, m_ref, l_ref, acc_ref, *, tile, lam_init, far_bucket):
    h = pl.program_id(1)
    i = pl.program_id(2)

    q = q_ref[...]
    lane = lax.broadcasted_iota(jnp.int32, q.shape, 1)
    qz_ref[0] = jnp.where(lane < D_HALF, q, jnp.zeros_like(q))
    qz_ref[1] = jnp.where(lane >= D_HALF, q, jnp.zeros_like(q))
    m_ref[...] = jnp.full(m_ref.shape, NEG_INF, F32)
    l_ref[...] = jnp.zeros(l_ref.shape, F32)
    acc_ref[...] = jnp.zeros(acc_ref.shape, F32)

    def step(j, bias):
        off = pl.multiple_of(j * tile, tile)
        kb = k_ref[pl.ds(off, tile), :]
        vtb = vt_ref[j]
        for mp in range(2):
            st = lax.dot_general(kb, qz_ref[mp], _NT, preferred_element_type=F32) + bias
            m_prev = m_ref[mp]
            m_new = jnp.maximum(m_prev, jnp.max(st, axis=0, keepdims=True))
            alpha = jnp.exp(m_prev - m_new)
            p = jnp.exp(st - m_new)
            l_ref[mp] = alpha * l_ref[mp] + jnp.sum(p, axis=0, keepdims=True)
            acc_ref[mp] = alpha * acc_ref[mp] + jnp.dot(vtb, p.astype(BF16), preferred_element_type=F32)
            m_ref[mp] = m_new

    far_bias = rb_ref[far_bucket, h]

    def far_body(j, carry):
        step(j, far_bias)
        return carry

    lax.fori_loop(0, jnp.maximum(i - 1, 0), far_body, 0)

    @pl.when(i >= 1)
    def _():
        step(i - 1, bias_ref[1])

    step(i, bias_ref[0])

    lam = _lam(lam_ref, lam_init)
    ot = acc_ref[0] / l_ref[0] - lam * (acc_ref[1] / l_ref[1])
    at = ot * lax.rsqrt(jnp.mean(ot * ot, axis=0, keepdims=True) + EPS) * gsub_ref[...]
    o_ref[...] = (at * (1.0 - lam_init)).T.astype(o_ref.dtype)


def _attn_prompt(qk, vt, bias_tiles, rel_bias, lam_params, g_subln, batch, seq, lam_init):
    tile = bias_tiles.shape[-1]
    assert tile >= MAX_DISTANCE and seq % tile == 0 and vt.shape[-1] == tile
    nq = seq // tile
    nb = rel_bias.shape[0]
    kernel = functools.partial(_attn_kernel, tile=tile, lam_init=lam_init, far_bucket=nb - 1)
    return pl.pallas_call(
        kernel,
        grid=(batch, N_HEADS, nq),
        in_specs=[
            pl.BlockSpec((tile, HEAD_DIM), lambda b, h, i: (b * nq + i, h)),
            pl.BlockSpec((seq, HEAD_DIM), lambda b, h, i: (b, N_HEADS + h)),
            pl.BlockSpec((nq, HEAD_DIM, tile), lambda b, h, i: (b, h, 0)),
            pl.BlockSpec((None, 2, tile, tile), lambda b, h, i: (h, 0, 0, 0)),
            pl.BlockSpec(memory_space=pltpu.SMEM),
            pl.BlockSpec((4, D_HALF), lambda b, h, i: (0, 0)),
            pl.BlockSpec((HEAD_DIM, 1), lambda b, h, i: (0, 0)),
        ],
        out_specs=pl.BlockSpec((tile, HEAD_DIM), lambda b, h, i: (b * nq + i, h)),
        out_shape=jax.ShapeDtypeStruct((batch * seq, ATTN_W), BF16),
        scratch_shapes=[
            pltpu.VMEM((2, tile, HEAD_DIM), BF16),
            pltpu.VMEM((2, 1, tile), F32),
            pltpu.VMEM((2, 1, tile), F32),
            pltpu.VMEM((2, HEAD_DIM, tile), F32),
        ],
        compiler_params=_params(3),
        name="attn_prompt",
    )(qk, qk, vt, bias_tiles, rel_bias, lam_params, g_subln.reshape(HEAD_DIM, 1))


def _sattn_kernel(pt_ref, q_ref, kn_ref, vn_ref, lam_ref, gsub_ref, rb_ref, *rest,
                  n_pages, page, t_new, lam_init, num_buckets):
    del pt_ref
    gp = PAGES_PER_STEP
    k_refs, v_refs = rest[:gp], rest[gp:2 * gp]
    o_ref, qs_ref, kbuf, vbuf, bias_ref, m_ref, l_ref, acc_ref = rest[2 * gp:]
    g = pl.program_id(1)
    n_steps = pl.num_programs(1)
    past = n_pages * page
    rows = 2 * SUBLANES
    blk = gp * page

    trow = lax.broadcasted_iota(jnp.int32, (rows, 1), 0) % SUBLANES

    @pl.when(g == 0)
    def _():
        q8 = q_ref[...]
        lane = lax.broadcasted_iota(jnp.int32, (SUBLANES, HEAD_DIM), 1)
        for hh in range(N_HEADS):
            qh = q8[:, hh * HEAD_DIM:(hh + 1) * HEAD_DIM]
            qs_ref[hh] = jnp.concatenate([jnp.where(lane < D_HALF, qh, 0.0),
                                          jnp.where(lane >= D_HALF, qh, 0.0)], axis=0)
        m_ref[...] = jnp.full(m_ref.shape, NEG_INF, F32)
        l_ref[...] = jnp.zeros(l_ref.shape, F32)
        acc_ref[...] = jnp.zeros(acc_ref.shape, F32)

    for j in range(gp):
        for hh in range(N_HEADS):
            kbuf[hh, j * page:(j + 1) * page, :] = k_refs[j][pl.ds(hh, page, stride=N_HEADS), :].astype(BF16)
            vbuf[hh, j * page:(j + 1) * page, :] = v_refs[j][pl.ds(hh, page, stride=N_HEADS), :].astype(BF16)

    near = past - (g + 1) * blk + 1 < MAX_DISTANCE

    @pl.when(near)
    def _():
        kpos = g * blk + lax.broadcasted_iota(jnp.int32, (rows, blk), 1)
        dist = past + trow - kpos
        bucket = _t5_bucket(dist, num_buckets)
        for hh in range(N_HEADS):
            b = jnp.zeros(dist.shape, F32)
            for j in range(num_buckets):
                b = jnp.where(bucket == j, rb_ref[j, hh], b)
            bias_ref[hh] = jnp.where(dist >= 0, b, NEG_INF)

    @pl.when(jnp.logical_not(near))
    def _():
        for hh in range(N_HEADS):
            bias_ref[hh] = jnp.full(bias_ref.shape[1:], rb_ref[num_buckets - 1, hh], F32)

    for hh in range(N_HEADS):
        s = lax.dot_general(qs_ref[hh].astype(BF16), kbuf[hh], _NT, preferred_element_type=F32)
        s = s + bias_ref[hh]
        m_prev = m_ref[hh]
        m_new = jnp.maximum(m_prev, jnp.max(s, axis=-1, keepdims=True))
        alpha = jnp.exp(m_prev - m_new)
        p = jnp.exp(s - m_new)
        l_ref[hh] = alpha * l_ref[hh] + jnp.sum(p, axis=-1, keepdims=True)
        acc_ref[hh] = alpha * acc_ref[hh] + jnp.dot(p.astype(BF16), vbuf[hh], preferred_element_type=F32)
        m_ref[hh] = m_new

    @pl.when(g == n_steps - 1)
    def _():
        lane = lax.broadcasted_iota(jnp.int32, (rows, LANES), 1)
        dist = trow - lane
        valid = (dist >= 0) & (lane < t_new)
        bucket = _t5_bucket(dist, num_buckets)
        lam = _lam(lam_ref, lam_init)
        for hh in range(N_HEADS):
            cols = slice(hh * HEAD_DIM, (hh + 1) * HEAD_DIM)
            qh = qs_ref[hh]
            s_new = jnp.zeros((rows, LANES), F32)
            b = jnp.zeros((rows, LANES), F32)
            for j in range(num_buckets):
                b = jnp.where(bucket == j, rb_ref[j, hh], b)
            for t in range(t_new):
                col = jnp.sum(qh * kn_ref[t:t + 1, cols], axis=-1, keepdims=True)
                s_new = jnp.where(lane == t, col, s_new)
            s_new = jnp.where(valid, s_new + b, NEG_INF)
            m_prev = m_ref[hh]
            m_fin = jnp.maximum(m_prev, jnp.max(s_new, axis=-1, keepdims=True))
            alpha = jnp.exp(m_prev - m_fin)
            p_new = jnp.exp(s_new - m_fin)
            l_fin = alpha * l_ref[hh] + jnp.sum(p_new, axis=-1, keepdims=True)
            acc = alpha * acc_ref[hh]
            for t in range(t_new):
                pt = jnp.sum(jnp.where(lane == t, p_new, 0.0), axis=-1, keepdims=True)
                acc = acc + pt * vn_ref[t:t + 1, cols]
            o = acc / l_fin
            diff = o[:SUBLANES] - lam * o[SUBLANES:]
            o_ref[:, cols] = _rms(diff, gsub_ref[...]) * (1.0 - lam_init)


def _attn_sample(q8, kn8, vn8, cache_k, cache_v, page_table, layer, rel_bias, lam_params, g_subln,
                 t_new, lam_init):
    nseq = q8.shape[0]
    depth, n_pool, page = cache_k.shape[:3]
    n_pages = page_table.shape[1]
    gp = PAGES_PER_STEP
    assert n_pages % gp == 0 and t_new <= SUBLANES
    ck = cache_k.reshape(depth * n_pool, page * N_HEADS, HEAD_DIM)
    cv = cache_v.reshape(depth * n_pool, page * N_HEADS, HEAD_DIM)
    base = layer * n_pool
    rows = 2 * SUBLANES
    blk = gp * page
    nb = rel_bias.shape[0]

    def page_spec(j):
        return pl.BlockSpec((None, page * N_HEADS, HEAD_DIM),
                            lambda b, g, pt: (base + pt[b * n_pages + g * gp + j], 0, 0))

    seq_spec = pl.BlockSpec((None, SUBLANES, ATTN_W), lambda b, g, pt: (b, 0, 0))
    kernel = functools.partial(_sattn_kernel, n_pages=n_pages, page=page, t_new=t_new,
                               lam_init=lam_init, num_buckets=nb)
    grid_spec = pltpu.PrefetchScalarGridSpec(
        num_scalar_prefetch=1,
        grid=(nseq, n_pages // gp),
        in_specs=[seq_spec, seq_spec, seq_spec,
                  pl.BlockSpec((4, D_HALF), lambda b, g, pt: (0, 0)),
                  pl.BlockSpec((1, HEAD_DIM), lambda b, g, pt: (0, 0)),
                  pl.BlockSpec(memory_space=pltpu.SMEM)]
                 + [page_spec(j) for j in range(gp)] + [page_spec(j) for j in range(gp)],
        out_specs=seq_spec,
        scratch_shapes=[
            pltpu.VMEM((N_HEADS, rows, HEAD_DIM), F32),
            pltpu.VMEM((N_HEADS, blk, HEAD_DIM), BF16),
            pltpu.VMEM((N_HEADS, blk, HEAD_DIM), BF16),
            pltpu.VMEM((N_HEADS, rows, blk), F32),
            pltpu.VMEM((N_HEADS, rows, 1), F32),
            pltpu.VMEM((N_HEADS, rows, 1), F32),
            pltpu.VMEM((N_HEADS, rows, HEAD_DIM), F32),
        ],
    )
    return pl.pallas_call(
        kernel,
        grid_spec=grid_spec,
        out_shape=jax.ShapeDtypeStruct((nseq, SUBLANES, ATTN_W), F32),
        compiler_params=_params(2),
        name="attn_sample",
    )(page_table.reshape(-1), q8, kn8, vn8, lam_params, g_subln, rel_bias,
      *([ck] * gp), *([cv] * gp))


def _mixer_kernel(a_ref, u_ref, gvn_ref, ga_ref, gb_ref, x_ref, ws_ref, bs_ref, wa_ref, wb_ref,
                  wo_ref, gpost_ref, o_ref, b_scr):
    tm = x_ref.shape[0]
    r = lax.broadcasted_iota(jnp.int32, (CHUNK, CHUNK), 0)
    c = lax.broadcasted_iota(jnp.int32, (CHUNK, CHUNK), 1)
    causal = r >= c
    for grp in range(N_GROUPS):
        cols = slice(grp * CHUNK, (grp + 1) * CHUNK)
        wg = ws_ref[grp]
        wg = jnp.where(causal, wg, jnp.zeros_like(wg))
        for ch in range(tm // CHUNK):
            rws = slice(ch * CHUNK, (ch + 1) * CHUNK)
            sp = jnp.dot(wg, gvn_ref[rws, cols].astype(BF16), preferred_element_type=F32)
            sp = sp + bs_ref[:, cols]
            b_scr[rws, cols] = (u_ref[rws, cols] * sp).astype(BF16)
    ya = jnp.dot(a_ref[...], wa_ref[...], preferred_element_type=F32)
    yb = jnp.dot(b_scr[...], wb_ref[...], preferred_element_type=F32)
    mix = jax.nn.sigmoid(ga_ref[...]) * ya + jax.nn.sigmoid(gb_ref[...]) * yb
    mo = jnp.dot(mix.astype(BF16), wo_ref[...], preferred_element_type=F32)
    o_ref[...] = x_ref[...] + _rms(mo, gpost_ref[...])


def _mixer(a, u, gvn, ga, gb, x, ws, bs_tile, w_a, w_b, w_out, g_post, tm):
    n, d = x.shape
    row = lambda w: pl.BlockSpec((tm, w), lambda i: (i, 0))
    return pl.pallas_call(
        _mixer_kernel,
        grid=(n // tm,),
        in_specs=[row(ATTN_W), row(GMLP_W), row(GMLP_W), row(d), row(d), row(d),
                  _resident(ws.shape), _resident(bs_tile.shape), _resident(w_a.shape),
                  _resident(w_b.shape), _resident(w_out.shape), _resident((1, d))],
        out_specs=row(d),
        out_shape=jax.ShapeDtypeStruct((n, d), F32),
        scratch_shapes=[pltpu.VMEM((tm, GMLP_W), BF16)],
        compiler_params=_params(1),
        name="mixer",
    )(a, u, gvn, ga, gb, x, ws, bs_tile, w_a, w_b, w_out, g_post)


def _ffn_kernel(x_ref, gpre_ref, w1_ref, w2_ref, gpost_ref, o_ref):
    d_ff = w2_ref.shape[0]
    x = x_ref[...]
    h = _rms(x, gpre_ref[...]).astype(BF16)
    acc = jnp.zeros(x.shape, F32)
    for c0 in range(0, d_ff, FF_TILE):
        gate = jnp.dot(h, w1_ref[:, c0:c0 + FF_TILE], preferred_element_type=F32)
        up = jnp.dot(h, w1_ref[:, d_ff + c0:d_ff + c0 + FF_TILE], preferred_element_type=F32)
        act = (gate * jax.nn.sigmoid(gate) * up).astype(BF16)
        acc = acc + jnp.dot(act, w2_ref[c0:c0 + FF_TILE, :], preferred_element_type=F32)
    o_ref[...] = x + _rms(acc, gpost_ref[...])


def _ffn(x, g_pre, w1, w2, g_post, tm):
    n, d = x.shape
    assert w2.shape[0] % FF_TILE == 0
    row = pl.BlockSpec((tm, d), lambda i: (i, 0))
    return pl.pallas_call(
        _ffn_kernel,
        grid=(n // tm,),
        in_specs=[row, _resident((1, d)), _resident(w1.shape), _resident(w2.shape), _resident((1, d))],
        out_specs=row,
        out_shape=jax.ShapeDtypeStruct((n, d), F32),
        compiler_params=_params(1),
        name="ffn",
    )(x, g_pre, w1, w2, g_post)


def kernel(x_prompt, x_sample, cache_k, cache_v, page_table, rel_bias, g_mix_pre, w_in, lam_q1, lam_k1, lam_q2, lam_k2, g_subln, g_gmlp_v, w_spatial, b_spatial, w_branch_a, w_branch_b, w_out, g_mix_post, g_ffn_pre, w_ffn_in, w_ffn_out, g_ffn_post):
    batch, seq, d = x_prompt.shape
    nseq, t_new, _ = x_sample.shape
    depth = w_in.shape[0]
    n_s = nseq * t_new
    assert n_s % CHUNK == 0 and CHUNK % t_new == 0
    attn_tile = min(ATTN_TILE, seq)
    tm_p = attn_tile
    tm_s = CHUNK

    bias_tiles = _bias_tiles(rel_bias, attn_tile)
    xp = x_prompt.reshape(batch * seq, d)
    xs = x_sample.reshape(n_s, d)
    row = lambda v: v.reshape(1, -1)
    eye = jnp.eye(CHUNK // t_new, dtype=F32)

    outs = {name: [] for name in ("kp", "vp", "ks", "vs", "gvs")}
    for l in range(depth):
        lam_init = 0.8 - 0.6 * math.exp(-0.3 * l)
        lam_params = jnp.stack([lam_q1[l], lam_k1[l], lam_q2[l], lam_k2[l]])
        w_in_l = w_in[l].astype(BF16)
        w_a, w_b, w_o = (w_branch_a[l].astype(BF16), w_branch_b[l].astype(BF16), w_out[l].astype(BF16))
        w1, w2 = w_ffn_in[l].astype(BF16), w_ffn_out[l].astype(BF16)
        gsub = row(g_subln[l])
        ws_p = w_spatial[l].astype(BF16)
        bs_p = jnp.repeat(b_spatial[l].T, CHUNK, axis=1)
        ws_s = jax.vmap(lambda w: jnp.kron(eye, w[:t_new, :t_new]))(w_spatial[l]).astype(BF16)
        bs_s = jnp.repeat(jnp.tile(b_spatial[l][:, :t_new], (1, CHUNK // t_new)).T, CHUNK, axis=1)

        qk, vt, kp, vp, u, gvn, ga, gb = _proj(xp, row(g_mix_pre[l]), w_in_l, row(g_gmlp_v[l]), tm_p)
        a = _attn_prompt(qk, vt, bias_tiles, rel_bias, lam_params, gsub, batch, seq, lam_init)
        xp = _mixer(a, u, gvn, ga, gb, xp, ws_p, bs_p, w_a, w_b, w_o, row(g_mix_post[l]), tm_p)
        xp = _ffn(xp, row(g_ffn_pre[l]), w1, w2, row(g_ffn_post[l]), tm_p)
        outs["kp"].append(kp.reshape(batch, seq, N_HEADS, HEAD_DIM))
        outs["vp"].append(vp.reshape(batch, seq, N_HEADS, HEAD_DIM))

        qk, _, ks, vs, u, gvn, ga, gb = _proj(xs, row(g_mix_pre[l]), w_in_l, row(g_gmlp_v[l]), tm_s)
        pad = lambda t: jnp.pad(t.astype(F32).reshape(nseq, t_new, ATTN_W),
                                ((0, 0), (0, SUBLANES - t_new), (0, 0)))
        a8 = _attn_sample(pad(qk[:, :ATTN_W]), pad(ks), pad(vs), cache_k, cache_v, page_table, l,
                          rel_bias, lam_params, gsub, t_new, lam_init)
        a = a8[:, :t_new].reshape(n_s, ATTN_W).astype(BF16)
        xs = _mixer(a, u, gvn, ga, gb, xs, ws_s, bs_s, w_a, w_b, w_o, row(g_mix_post[l]), tm_s)
        xs = _ffn(xs, row(g_ffn_pre[l]), w1, w2, row(g_ffn_post[l]), tm_s)
        outs["ks"].append(ks.reshape(nseq, t_new, N_HEADS, HEAD_DIM))
        outs["vs"].append(vs.reshape(nseq, t_new, N_HEADS, HEAD_DIM))
        outs["gvs"].append(gvn.reshape(nseq, t_new, GMLP_W))

    return (xp.reshape(batch, seq, d), xs.reshape(nseq, t_new, d),
            jnp.stack(outs["kp"]), jnp.stack(outs["vp"]),
            jnp.stack(outs["ks"]), jnp.stack(outs["vs"]), jnp.stack(outs["gvs"]))
```

```python
import functools
import math

import jax
import jax.numpy as jnp
from jax import lax
from jax.experimental import pallas as pl
from jax.experimental.pallas import tpu as pltpu

F32 = jnp.float32
BF16 = jnp.bfloat16

LANES = 128
SUBLANES = 8
BF16_ROWS = 16
VMEM_LIMIT_BYTES = 56 * 1024 * 1024

N_HEADS = 4
D_HALF = 64
HEAD_DIM = 2 * D_HALF
ATTN_W = N_HEADS * HEAD_DIM
N_GROUPS = 4
CHUNK = 128
GMLP_W = N_GROUPS * CHUNK
SCALE = D_HALF ** -0.5
MAX_EXACT = 16
MAX_DISTANCE = 128
EPS = 1e-6
NEG_INF = -1e30
LOG2E = math.log2(math.e)
VT_ROWS = HEAD_DIM + BF16_ROWS

ATTN_TILE = 512
KEY_CHUNK = 256
PAGES_PER_STEP = 8
FF_TILE = 256

_NT = (((1,), (1,)), ((), ()))


def _rms(x, g):
    return x * lax.rsqrt(jnp.mean(x * x, axis=-1, keepdims=True) + EPS) * g


def _t5_bucket(dist, num_buckets):
    n = jnp.maximum(dist, 0)
    nf = jnp.maximum(n, 1).astype(F32)
    large = MAX_EXACT + (jnp.log(nf / MAX_EXACT) / math.log(MAX_DISTANCE / MAX_EXACT)
                         * (num_buckets - MAX_EXACT)).astype(jnp.int32)
    return jnp.where(n < MAX_EXACT, n, jnp.minimum(large, num_buckets - 1))


def _resident(shape):
    return pl.BlockSpec(shape, lambda *_: (0,) * len(shape), pipeline_mode=pl.Buffered(1))


def _params(n_axes):
    return pltpu.CompilerParams(dimension_semantics=("arbitrary",) * n_axes,
                                vmem_limit_bytes=VMEM_LIMIT_BYTES)


def _proj_kernel(x_ref, g_ref, w_ref, ggv_ref, qk_ref, vt_ref, k_ref, v_ref, u_ref, gvn_ref, ga_ref, gb_ref):
    tm, d_model = x_ref.shape
    h = _rms(x_ref[...], g_ref[...]).astype(BF16)

    def seg(start, width):
        return jnp.dot(h, w_ref[:, start:start + width], preferred_element_type=F32)

    q = seg(0, ATTN_W)
    k = seg(ATTN_W, ATTN_W)
    v = seg(2 * ATTN_W, ATTN_W)
    qk_ref[:, 0:ATTN_W] = (q * (SCALE * LOG2E)).astype(BF16)
    qk_ref[:, ATTN_W:2 * ATTN_W] = k.astype(BF16)
    vt = v.T.astype(BF16)
    pad_row = lax.broadcasted_iota(jnp.int32, (VT_ROWS - HEAD_DIM, tm), 0)
    ones_rows = jnp.where(pad_row == 0, 1.0, 0.0).astype(BF16)
    for hh in range(N_HEADS):
        vt_ref[hh * VT_ROWS:hh * VT_ROWS + HEAD_DIM, :] = vt[hh * HEAD_DIM:(hh + 1) * HEAD_DIM, :]
        vt_ref[hh * VT_ROWS + HEAD_DIM:(hh + 1) * VT_ROWS, :] = ones_rows
        k_ref[pl.ds(hh, tm, stride=N_HEADS), :] = k[:, hh * HEAD_DIM:(hh + 1) * HEAD_DIM]
        v_ref[pl.ds(hh, tm, stride=N_HEADS), :] = v[:, hh * HEAD_DIM:(hh + 1) * HEAD_DIM]
    u_ref[...] = seg(3 * ATTN_W, GMLP_W)
    gvn_ref[...] = _rms(seg(3 * ATTN_W + GMLP_W, GMLP_W), ggv_ref[...])
    ga_ref[...] = seg(3 * ATTN_W + 2 * GMLP_W, d_model)
    gb_ref[...] = seg(3 * ATTN_W + 2 * GMLP_W + d_model, d_model)


def _proj(x, g_pre, w_in, g_gv, tm):
    n, d = x.shape
    in_w = w_in.shape[1]
    row = lambda w: pl.BlockSpec((tm, w), lambda i: (i, 0))
    head_rows = pl.BlockSpec((tm * N_HEADS, HEAD_DIM), lambda i: (i, 0))
    widths = (GMLP_W, GMLP_W, d, d)
    return pl.pallas_call(
        _proj_kernel,
        grid=(n // tm,),
        in_specs=[row(d), _resident((1, d)), _resident((d, in_w)), _resident((1, GMLP_W))],
        out_specs=[row(2 * ATTN_W), pl.BlockSpec((None, N_HEADS * VT_ROWS, tm), lambda i: (i, 0, 0)),
                   head_rows, head_rows] + [row(w) for w in widths],
        out_shape=[jax.ShapeDtypeStruct((n, 2 * ATTN_W), BF16),
                   jax.ShapeDtypeStruct((n // tm, N_HEADS * VT_ROWS, tm), BF16),
                   jax.ShapeDtypeStruct((n * N_HEADS, HEAD_DIM), F32),
                   jax.ShapeDtypeStruct((n * N_HEADS, HEAD_DIM), F32)]
                  + [jax.ShapeDtypeStruct((n, w), F32) for w in widths],
        compiler_params=_params(1),
        name="proj",
    )(x, g_pre, w_in, g_gv)


def _bias_tile_kernel(rb_ref, o_ref, *, tile, num_buckets):
    h = pl.program_id(0)
    off = pl.program_id(1)
    key = lax.broadcasted_iota(jnp.int32, (tile, tile), 0)
    qry = lax.broadcasted_iota(jnp.int32, (tile, tile), 1)
    dist = off * tile + qry - key
    bucket = _t5_bucket(dist, num_buckets)
    bias = jnp.zeros((tile, tile), F32)
    for j in range(num_buckets):
        bias = jnp.where(bucket == j, rb_ref[j, h], bias)
    o_ref[...] = jnp.where(dist >= 0, bias * LOG2E, NEG_INF)


def _bias_tiles(rel_bias, tile):
    nb, nh = rel_bias.shape
    return pl.pallas_call(
        functools.partial(_bias_tile_kernel, tile=tile, num_buckets=nb),
        grid=(nh, 2),
        in_specs=[pl.BlockSpec(memory_space=pltpu.SMEM)],
        out_specs=pl.BlockSpec((None, None, tile, tile), lambda h, o: (h, o, 0, 0)),
        out_shape=jax.ShapeDtypeStruct((nh, 2, tile, tile), F32),
        compiler_params=_params(2),
        name="bias_tiles",
    )(rel_bias)


def _lam(lam_ref, lam_init):
    lp = lam_ref[...]
    s1 = jnp.sum(lp[0:1] * lp[1:2], axis=-1, keepdims=True)
    s2 = jnp.sum(lp[2:3] * lp[3:4], axis=-1, keepdims=True)
    return jnp.exp(s1) - jnp.exp(s2) + lam_init


def _attn_kernel(q_ref, k_ref, vt_ref, bias_ref, rb_ref, lam_ref, gsub_ref, o_ref,
                 qz_ref, m_ref, acc_ref, *, tile, lam_init, far_bucket):
    h = pl.program_id(1)
    i = pl.program_id(2)

    q = q_ref[...]
    lane = lax.broadcasted_iota(jnp.int32, q.shape, 1)
    qz_ref[0] = jnp.where(lane < D_HALF, q, jnp.zeros_like(q))
    qz_ref[1] = jnp.where(lane >= D_HALF, q, jnp.zeros_like(q))
    m_ref[...] = jnp.full(m_ref.shape, NEG_INF, F32)
    acc_ref[...] = jnp.zeros(acc_ref.shape, F32)

    def step(j, bias_tile, bias_const):
        off = pl.multiple_of(j * tile, tile)
        chunks = [(k0, mp) for k0 in range(0, tile, KEY_CHUNK) for mp in range(2)]

        def scores(k0, mp):
            kb = k_ref[pl.ds(off + k0, KEY_CHUNK), :]
            return lax.dot_general(kb, qz_ref[mp], _NT, preferred_element_type=F32)

        st_next = scores(*chunks[0])
        for c, (k0, mp) in enumerate(chunks):
            st = st_next
            if c + 1 < len(chunks):
                st_next = scores(*chunks[c + 1])
            if bias_tile is not None:
                st = st + bias_ref[bias_tile, k0:k0 + KEY_CHUNK, :]
            cmax = jnp.max(st, axis=0, keepdims=True)
            if bias_const is not None:
                cmax = cmax + bias_const
            m_prev = m_ref[mp]
            m_new = jnp.maximum(m_prev, cmax)
            alpha = jnp.exp2(m_prev - m_new)
            p = jnp.exp2(st - (m_new if bias_const is None else m_new - bias_const))
            vtb = vt_ref[j, :, k0:k0 + KEY_CHUNK]
            acc_ref[mp] = alpha * acc_ref[mp] + jnp.dot(vtb, p.astype(BF16), preferred_element_type=F32)
            m_ref[mp] = m_new

    far_bias = rb_ref[far_bucket, h] * LOG2E

    def far_body(j, carry):
        step(j, None, far_bias)
        return carry

    lax.fori_loop(0, jnp.maximum(i - 1, 0), far_body, 0)

    @pl.when(i >= 1)
    def _():
        step(i - 1, 1, None)

    step(i, 0, None)

    lam = _lam(lam_ref, lam_init)
    a1, a2 = acc_ref[0], acc_ref[1]
    ot = (a1[:HEAD_DIM] / a1[HEAD_DIM:HEAD_DIM + 1]
          - lam * (a2[:HEAD_DIM] / a2[HEAD_DIM:HEAD_DIM + 1]))
    at = ot * lax.rsqrt(jnp.mean(ot * ot, axis=0, keepdims=True) + EPS) * gsub_ref[...]
    o_ref[...] = (at * (1.0 - lam_init)).T.astype(o_ref.dtype)


def _attn_prompt(qk, vt, bias_tiles, rel_bias, lam_params, g_subln, batch, seq, lam_init):
    tile = bias_tiles.shape[-1]
    assert tile >= MAX_DISTANCE and seq % tile == 0 and vt.shape[-1] == tile
    nq = seq // tile
    nb = rel_bias.shape[0]
    kernel = functools.partial(_attn_kernel, tile=tile, lam_init=lam_init, far_bucket=nb - 1)
    return pl.pallas_call(
        kernel,
        grid=(batch, N_HEADS, nq),
        in_specs=[
            pl.BlockSpec((tile, HEAD_DIM), lambda b, h, i: (b * nq + i, h)),
            pl.BlockSpec((seq, HEAD_DIM), lambda b, h, i: (b, N_HEADS + h)),
            pl.BlockSpec((nq, VT_ROWS, tile), lambda b, h, i: (b, h, 0)),
            pl.BlockSpec((None, 2, tile, tile), lambda b, h, i: (h, 0, 0, 0)),
            pl.BlockSpec(memory_space=pltpu.SMEM),
            pl.BlockSpec((4, D_HALF), lambda b, h, i: (0, 0)),
            pl.BlockSpec((HEAD_DIM, 1), lambda b, h, i: (0, 0)),
        ],
        out_specs=pl.BlockSpec((tile, HEAD_DIM), lambda b, h, i: (b * nq + i, h)),
        out_shape=jax.ShapeDtypeStruct((batch * seq, ATTN_W), BF16),
        scratch_shapes=[
            pltpu.VMEM((2, tile, HEAD_DIM), BF16),
            pltpu.VMEM((2, 1, tile), F32),
            pltpu.VMEM((2, VT_ROWS, tile), F32),
        ],
        compiler_params=_params(3),
        name="attn_prompt",
    )(qk, qk, vt, bias_tiles, rel_bias, lam_params, g_subln.reshape(HEAD_DIM, 1))


def _sattn_kernel(pt_ref, q_ref, kn_ref, vn_ref, lam_ref, gsub_ref, rb_ref, *rest,
                  page, t_new, lam_init, num_buckets):
    del pt_ref
    gp = PAGES_PER_STEP
    k_refs, v_refs = rest[:gp], rest[gp:2 * gp]
    o_ref, qs_ref, kbuf, vbuf, bias_ref, m_ref, l_ref, acc_ref = rest[2 * gp:]
    g = pl.program_id(1)
    n_steps = pl.num_programs(1)
    head_rows = 2 * SUBLANES
    rows = N_HEADS * head_rows
    pcols = page * N_HEADS
    cols = gp * pcols

    r = lax.broadcasted_iota(jnp.int32, (rows, 1), 0)
    hrow = r // head_rows
    trow = r % SUBLANES

    def bias_column(bucket):
        col = jnp.full((rows, 1), rb_ref[bucket, 0], F32)
        for hh in range(1, N_HEADS):
            col = jnp.where(hrow == hh, rb_ref[bucket, hh], col)
        return col * LOG2E

    @pl.when(g == 0)
    def _():
        q8 = q_ref[...]
        lane = lax.broadcasted_iota(jnp.int32, (SUBLANES, HEAD_DIM), 1)
        for hh in range(N_HEADS):
            qh = q8[:, hh * HEAD_DIM:(hh + 1) * HEAD_DIM]
            qs_ref[hh * head_rows:hh * head_rows + SUBLANES, :] = jnp.where(lane < D_HALF, qh, 0.0)
            qs_ref[hh * head_rows + SUBLANES:(hh + 1) * head_rows, :] = jnp.where(lane >= D_HALF, qh, 0.0)
        m_ref[...] = jnp.full(m_ref.shape, NEG_INF, F32)
        l_ref[...] = jnp.zeros(l_ref.shape, F32)
        acc_ref[...] = jnp.zeros(acc_ref.shape, F32)
        chead = lax.broadcasted_iota(jnp.int32, (rows, cols), 1) % N_HEADS
        bias_ref[...] = jnp.where(chead == hrow, bias_column(num_buckets - 1), NEG_INF)

    @pl.when(g == n_steps - 1)
    def _():
        c = lax.broadcasted_iota(jnp.int32, (rows, pcols), 1)
        dist = trow + (page - c // N_HEADS)
        bucket = _t5_bucket(dist, num_buckets)
        b = jnp.zeros(dist.shape, F32)
        for j in range(num_buckets):
            b = jnp.where(bucket == j, bias_column(j), b)
        ok = (c % N_HEADS == hrow) & (dist >= 0)
        bias_ref[:, cols - pcols:] = jnp.where(ok, b, NEG_INF)

    for j in range(gp):
        kbuf[j * pcols:(j + 1) * pcols, :] = k_refs[j][...].astype(BF16)
        vbuf[j * pcols:(j + 1) * pcols, :] = v_refs[j][...].astype(BF16)

    s = lax.dot_general(qs_ref[...].astype(BF16), kbuf[...], _NT, preferred_element_type=F32)
    s = s + bias_ref[...]
    m_prev = m_ref[...]
    m_new = jnp.maximum(m_prev, jnp.max(s, axis=-1, keepdims=True))
    alpha = jnp.exp2(m_prev - m_new)
    p = jnp.exp2(s - m_new)
    l_ref[...] = alpha * l_ref[...] + jnp.sum(p, axis=-1, keepdims=True)
    acc_ref[...] = alpha * acc_ref[...] + jnp.dot(p.astype(BF16), vbuf[...], preferred_element_type=F32)
    m_ref[...] = m_new

    @pl.when(g == n_steps - 1)
    def _():
        lane = lax.broadcasted_iota(jnp.int32, (head_rows, LANES), 1)
        dist = lax.broadcasted_iota(jnp.int32, (head_rows, 1), 0) % SUBLANES - lane
        valid = (dist >= 0) & (lane < t_new)
        bucket = _t5_bucket(dist, num_buckets)
        lam = _lam(lam_ref, lam_init)
        for hh in range(N_HEADS):
            rws = slice(hh * head_rows, (hh + 1) * head_rows)
            b = jnp.zeros((head_rows, LANES), F32)
            for j in range(min(num_buckets, MAX_EXACT)):
                b = jnp.where(bucket == j, rb_ref[j, hh] * LOG2E, b)
            qh = qs_ref[rws, :]
            s_new = jnp.zeros((head_rows, LANES), F32)
            for t in range(t_new):
                krow = kn_ref[t * N_HEADS + hh:t * N_HEADS + hh + 1, :]
                s_new = jnp.where(lane == t, jnp.sum(qh * krow, axis=-1, keepdims=True), s_new)
            s_new = jnp.where(valid, s_new + b, NEG_INF)
            m_prev = m_ref[rws, :]
            m_fin = jnp.maximum(m_prev, jnp.max(s_new, axis=-1, keepdims=True))
            alpha = jnp.exp2(m_prev - m_fin)
            p_new = jnp.exp2(s_new - m_fin)
            l_fin = alpha * l_ref[rws, :] + jnp.sum(p_new, axis=-1, keepdims=True)
            acc = alpha * acc_ref[rws, :]
            for t in range(t_new):
                pt = jnp.sum(jnp.where(lane == t, p_new, 0.0), axis=-1, keepdims=True)
                acc = acc + pt * vn_ref[t * N_HEADS + hh:t * N_HEADS + hh + 1, :]
            o = acc / l_fin
            diff = o[:SUBLANES] - lam * o[SUBLANES:]
            o_ref[:, hh * HEAD_DIM:(hh + 1) * HEAD_DIM] = _rms(diff, gsub_ref[...]) * (1.0 - lam_init)


def _attn_sample(q8, kn, vn, cache_k, cache_v, page_table, layer, rel_bias, lam_params, g_subln,
                 t_new, lam_init):
    nseq = q8.shape[0]
    depth, n_pool, page = cache_k.shape[:3]
    n_pages = page_table.shape[1]
    gp = PAGES_PER_STEP
    assert n_pages % gp == 0 and t_new <= SUBLANES
    assert page >= MAX_DISTANCE and rel_bias.shape[0] >= MAX_EXACT
    ck = cache_k.reshape(depth * n_pool, page * N_HEADS, HEAD_DIM)
    cv = cache_v.reshape(depth * n_pool, page * N_HEADS, HEAD_DIM)
    base = layer * n_pool
    rows = N_HEADS * 2 * SUBLANES
    cols = gp * page * N_HEADS
    nb = rel_bias.shape[0]

    def page_spec(j):
        return pl.BlockSpec((None, page * N_HEADS, HEAD_DIM),
                            lambda b, g, pt: (base + pt[b * n_pages + g * gp + j], 0, 0))

    seq_spec = pl.BlockSpec((None, SUBLANES, ATTN_W), lambda b, g, pt: (b, 0, 0))
    new_spec = pl.BlockSpec((None, SUBLANES * N_HEADS, HEAD_DIM), lambda b, g, pt: (b, 0, 0))
    kernel = functools.partial(_sattn_kernel, page=page, t_new=t_new, lam_init=lam_init, num_buckets=nb)
    grid_spec = pltpu.PrefetchScalarGridSpec(
        num_scalar_prefetch=1,
        grid=(nseq, n_pages // gp),
        in_specs=[seq_spec, new_spec, new_spec,
                  pl.BlockSpec((4, D_HALF), lambda b, g, pt: (0, 0)),
                  pl.BlockSpec((1, HEAD_DIM), lambda b, g, pt: (0, 0)),
                  pl.BlockSpec(memory_space=pltpu.SMEM)]
                 + [page_spec(j) for j in range(gp)] + [page_spec(j) for j in range(gp)],
        out_specs=seq_spec,
        scratch_shapes=[
            pltpu.VMEM((rows, HEAD_DIM), F32),
            pltpu.VMEM((cols, HEAD_DIM), BF16),
            pltpu.VMEM((cols, HEAD_DIM), BF16),
            pltpu.VMEM((rows, cols), F32),
            pltpu.VMEM((rows, 1), F32),
            pltpu.VMEM((rows, 1), F32),
            pltpu.VMEM((rows, HEAD_DIM), F32),
        ],
    )
    return pl.pallas_call(
        kernel,
        grid_spec=grid_spec,
        out_shape=jax.ShapeDtypeStruct((nseq, SUBLANES, ATTN_W), F32),
        compiler_params=_params(2),
        name="attn_sample",
    )(page_table.reshape(-1), q8, kn, vn, lam_params, g_subln, rel_bias,
      *([ck] * gp), *([cv] * gp))


def _mixer_kernel(a_ref, u_ref, gvn_ref, ga_ref, gb_ref, x_ref, ws_ref, bs_ref, wa_ref, wb_ref,
                  wo_ref, gpost_ref, o_ref, b_scr):
    tm = x_ref.shape[0]
    r = lax.broadcasted_iota(jnp.int32, (CHUNK, CHUNK), 0)
    c = lax.broadcasted_iota(jnp.int32, (CHUNK, CHUNK), 1)
    causal = r >= c
    for grp in range(N_GROUPS):
        cols = slice(grp * CHUNK, (grp + 1) * CHUNK)
        wg = ws_ref[grp]
        wg = jnp.where(causal, wg, jnp.zeros_like(wg))
        for ch in range(tm // CHUNK):
            rws = slice(ch * CHUNK, (ch + 1) * CHUNK)
            sp = jnp.dot(wg, gvn_ref[rws, cols].astype(BF16), preferred_element_type=F32)
            sp = sp + bs_ref[:, cols]
            b_scr[rws, cols] = (u_ref[rws, cols] * sp).astype(BF16)
    ya = jnp.dot(a_ref[...], wa_ref[...], preferred_element_type=F32)
    yb = jnp.dot(b_scr[...], wb_ref[...], preferred_element_type=F32)
    mix = jax.nn.sigmoid(ga_ref[...]) * ya + jax.nn.sigmoid(gb_ref[...]) * yb
    mo = jnp.dot(mix.astype(BF16), wo_ref[...], preferred_element_type=F32)
    o_ref[...] = x_ref[...] + _rms(mo, gpost_ref[...])


def _mixer(a, u, gvn, ga, gb, x, ws, bs_tile, w_a, w_b, w_out, g_post, tm):
    n, d = x.shape
    row = lambda w: pl.BlockSpec((tm, w), lambda i: (i, 0))
    return pl.pallas_call(
        _mixer_kernel,
        grid=(n // tm,),
        in_specs=[row(ATTN_W), row(GMLP_W), row(GMLP_W), row(d), row(d), row(d),
                  _resident(ws.shape), _resident(bs_tile.shape), _resident(w_a.shape),
                  _resident(w_b.shape), _resident(w_out.shape), _resident((1, d))],
        out_specs=row(d),
        out_shape=jax.ShapeDtypeStruct((n, d), F32),
        scratch_shapes=[pltpu.VMEM((tm, GMLP_W), BF16)],
        compiler_params=_params(1),
        name="mixer",
    )(a, u, gvn, ga, gb, x, ws, bs_tile, w_a, w_b, w_out, g_post)


def _ffn_kernel(x_ref, gpre_ref, w1_ref, w2_ref, gpost_ref, o_ref):
    d_ff = w2_ref.shape[0]
    x = x_ref[...]
    h = _rms(x, gpre_ref[...]).astype(BF16)
    acc = jnp.zeros(x.shape, F32)
    for c0 in range(0, d_ff, FF_TILE):
        gate = jnp.dot(h, w1_ref[:, c0:c0 + FF_TILE], preferred_element_type=F32)
        up = jnp.dot(h, w1_ref[:, d_ff + c0:d_ff + c0 + FF_TILE], preferred_element_type=F32)
        act = (gate * jax.nn.sigmoid(gate) * up).astype(BF16)
        acc = acc + jnp.dot(act, w2_ref[c0:c0 + FF_TILE, :], preferred_element_type=F32)
    o_ref[...] = x + _rms(acc, gpost_ref[...])


def _ffn(x, g_pre, w1, w2, g_post, tm):
    n, d = x.shape
    assert w2.shape[0] % FF_TILE == 0
    row = pl.BlockSpec((tm, d), lambda i: (i, 0))
    return pl.pallas_call(
        _ffn_kernel,
        grid=(n // tm,),
        in_specs=[row, _resident((1, d)), _resident(w1.shape), _resident(w2.shape), _resident((1, d))],
        out_specs=row,
        out_shape=jax.ShapeDtypeStruct((n, d), F32),
        compiler_params=_params(1),
        name="ffn",
    )(x, g_pre, w1, w2, g_post)


def kernel(x_prompt, x_sample, cache_k, cache_v, page_table, rel_bias, g_mix_pre, w_in, lam_q1, lam_k1, lam_q2, lam_k2, g_subln, g_gmlp_v, w_spatial, b_spatial, w_branch_a, w_branch_b, w_out, g_mix_post, g_ffn_pre, w_ffn_in, w_ffn_out, g_ffn_post):
    batch, seq, d = x_prompt.shape
    nseq, t_new, _ = x_sample.shape
    depth = w_in.shape[0]
    n_s = nseq * t_new
    assert n_s % CHUNK == 0 and CHUNK % t_new == 0
    attn_tile = min(ATTN_TILE, seq)
    tm_p = attn_tile
    tm_s = CHUNK

    bias_tiles = _bias_tiles(rel_bias, attn_tile)
    xp = x_prompt.reshape(batch * seq, d)
    xs = x_sample.reshape(n_s, d)
    row = lambda v: v.reshape(1, -1)
    eye = jnp.eye(CHUNK // t_new, dtype=F32)

    outs = {name: [] for name in ("kp", "vp", "ks", "vs", "gvs")}
    for l in range(depth):
        lam_init = 0.8 - 0.6 * math.exp(-0.3 * l)
        lam_params = jnp.stack([lam_q1[l], lam_k1[l], lam_q2[l], lam_k2[l]])
        w_in_l = w_in[l].astype(BF16)
        w_a, w_b, w_o = (w_branch_a[l].astype(BF16), w_branch_b[l].astype(BF16), w_out[l].astype(BF16))
        w1, w2 = w_ffn_in[l].astype(BF16), w_ffn_out[l].astype(BF16)
        gsub = row(g_subln[l])
        ws_p = w_spatial[l].astype(BF16)
        bs_p = jnp.repeat(b_spatial[l].T, CHUNK, axis=1)
        ws_s = jax.vmap(lambda w: jnp.kron(eye, w[:t_new, :t_new]))(w_spatial[l]).astype(BF16)
        bs_s = jnp.repeat(jnp.tile(b_spatial[l][:, :t_new], (1, CHUNK // t_new)).T, CHUNK, axis=1)

        qk, vt, kp, vp, u, gvn, ga, gb = _proj(xp, row(g_mix_pre[l]), w_in_l, row(g_gmlp_v[l]), tm_p)
        a = _attn_prompt(qk, vt, bias_tiles, rel_bias, lam_params, gsub, batch, seq, lam_init)
        xp = _mixer(a, u, gvn, ga, gb, xp, ws_p, bs_p, w_a, w_b, w_o, row(g_mix_post[l]), tm_p)
        xp = _ffn(xp, row(g_ffn_pre[l]), w1, w2, row(g_ffn_post[l]), tm_p)
        outs["kp"].append(kp.reshape(batch, seq, N_HEADS, HEAD_DIM))
        outs["vp"].append(vp.reshape(batch, seq, N_HEADS, HEAD_DIM))

        qk, _, ks, vs, u, gvn, ga, gb = _proj(xs, row(g_mix_pre[l]), w_in_l, row(g_gmlp_v[l]), tm_s)
        pad_tokens = lambda t: jnp.pad(t, ((0, 0), (0, SUBLANES - t_new)) + ((0, 0),) * (t.ndim - 2))
        q8 = pad_tokens(qk[:, :ATTN_W].astype(F32).reshape(nseq, t_new, ATTN_W))
        kn = pad_tokens(ks.reshape(nseq, t_new, ATTN_W)).reshape(nseq, SUBLANES * N_HEADS, HEAD_DIM)
        vn = pad_tokens(vs.reshape(nseq, t_new, ATTN_W)).reshape(nseq, SUBLANES * N_HEADS, HEAD_DIM)
        a8 = _attn_sample(q8, kn, vn, cache_k, cache_v, page_table, l, rel_bias, lam_params, gsub,
                          t_new, lam_init)
        a = a8[:, :t_new].reshape(n_s, ATTN_W).astype(BF16)
        xs = _mixer(a, u, gvn, ga, gb, xs, ws_s, bs_s, w_a, w_b, w_o, row(g_mix_post[l]), tm_s)
        xs = _ffn(xs, row(g_ffn_pre[l]), w1, w2, row(g_ffn_post[l]), tm_s)
        outs["ks"].append(ks.reshape(nseq, t_new, N_HEADS, HEAD_DIM))
        outs["vs"].append(vs.reshape(nseq, t_new, N_HEADS, HEAD_DIM))
        outs["gvs"].append(gvn.reshape(nseq, t_new, GMLP_W))

    return (xp.reshape(batch, seq, d), xs.reshape(nseq, t_new, d),
            jnp.stack(outs["kp"]), jnp.stack(outs["vp"]),
            jnp.stack(outs["ks"]), jnp.stack(outs["vs"]), jnp.stack(outs["gvs"]))
```

```python
import functools
import math

import jax
import jax.numpy as jnp
from jax import lax
from jax.experimental import pallas as pl
from jax.experimental.pallas import tpu as pltpu

F32 = jnp.float32
BF16 = jnp.bfloat16

LANES = 128
SUBLANES = 8
BF16_ROWS = 16
VMEM_LIMIT_BYTES = 56 * 1024 * 1024

N_HEADS = 4
D_HALF = 64
HEAD_DIM = 2 * D_HALF
ATTN_W = N_HEADS * HEAD_DIM
N_GROUPS = 4
CHUNK = 128
GMLP_W = N_GROUPS * CHUNK
SCALE = D_HALF ** -0.5
MAX_EXACT = 16
MAX_DISTANCE = 128
EPS = 1e-6
NEG_INF = -1e30
LOG2E = math.log2(math.e)
VT_ROWS = HEAD_DIM + BF16_ROWS

ATTN_TILE = 512
PAGES_PER_STEP = 8
FF_TILE = 256

_NT = (((1,), (1,)), ((), ()))


def _rms(x, g):
    return x * lax.rsqrt(jnp.mean(x * x, axis=-1, keepdims=True) + EPS) * g


def _t5_bucket(dist, num_buckets):
    n = jnp.maximum(dist, 0)
    nf = jnp.maximum(n, 1).astype(F32)
    large = MAX_EXACT + (jnp.log(nf / MAX_EXACT) / math.log(MAX_DISTANCE / MAX_EXACT)
                         * (num_buckets - MAX_EXACT)).astype(jnp.int32)
    return jnp.where(n < MAX_EXACT, n, jnp.minimum(large, num_buckets - 1))


def _resident(shape):
    return pl.BlockSpec(shape, lambda *_: (0,) * len(shape), pipeline_mode=pl.Buffered(1))


def _params(n_axes):
    return pltpu.CompilerParams(dimension_semantics=("arbitrary",) * n_axes,
                                vmem_limit_bytes=VMEM_LIMIT_BYTES)


def _proj_kernel(x_ref, g_ref, w_ref, ggv_ref, qk_ref, vt_ref, k_ref, v_ref, u_ref, gvn_ref, ga_ref, gb_ref):
    tm, d_model = x_ref.shape
    h = _rms(x_ref[...], g_ref[...]).astype(BF16)

    def seg(start, width):
        return jnp.dot(h, w_ref[:, start:start + width], preferred_element_type=F32)

    q = seg(0, ATTN_W)
    k = seg(ATTN_W, ATTN_W)
    v = seg(2 * ATTN_W, ATTN_W)
    qk_ref[:, 0:ATTN_W] = (q * (SCALE * LOG2E)).astype(BF16)
    qk_ref[:, ATTN_W:2 * ATTN_W] = k.astype(BF16)
    vt = v.T.astype(BF16)
    pad_row = lax.broadcasted_iota(jnp.int32, (VT_ROWS - HEAD_DIM, tm), 0)
    ones_rows = jnp.where(pad_row == 0, 1.0, 0.0).astype(BF16)
    for hh in range(N_HEADS):
        vt_ref[hh * VT_ROWS:hh * VT_ROWS + HEAD_DIM, :] = vt[hh * HEAD_DIM:(hh + 1) * HEAD_DIM, :]
        vt_ref[hh * VT_ROWS + HEAD_DIM:(hh + 1) * VT_ROWS, :] = ones_rows
        k_ref[pl.ds(hh, tm, stride=N_HEADS), :] = k[:, hh * HEAD_DIM:(hh + 1) * HEAD_DIM]
        v_ref[pl.ds(hh, tm, stride=N_HEADS), :] = v[:, hh * HEAD_DIM:(hh + 1) * HEAD_DIM]
    u_ref[...] = seg(3 * ATTN_W, GMLP_W)
    gvn_ref[...] = _rms(seg(3 * ATTN_W + GMLP_W, GMLP_W), ggv_ref[...])
    ga_ref[...] = seg(3 * ATTN_W + 2 * GMLP_W, d_model)
    gb_ref[...] = seg(3 * ATTN_W + 2 * GMLP_W + d_model, d_model)


def _proj(x, g_pre, w_in, g_gv, tm):
    n, d = x.shape
    in_w = w_in.shape[1]
    row = lambda w: pl.BlockSpec((tm, w), lambda i: (i, 0))
    head_rows = pl.BlockSpec((tm * N_HEADS, HEAD_DIM), lambda i: (i, 0))
    widths = (GMLP_W, GMLP_W, d, d)
    return pl.pallas_call(
        _proj_kernel,
        grid=(n // tm,),
        in_specs=[row(d), _resident((1, d)), _resident((d, in_w)), _resident((1, GMLP_W))],
        out_specs=[row(2 * ATTN_W), pl.BlockSpec((None, N_HEADS * VT_ROWS, tm), lambda i: (i, 0, 0)),
                   head_rows, head_rows] + [row(w) for w in widths],
        out_shape=[jax.ShapeDtypeStruct((n, 2 * ATTN_W), BF16),
                   jax.ShapeDtypeStruct((n // tm, N_HEADS * VT_ROWS, tm), BF16),
                   jax.ShapeDtypeStruct((n * N_HEADS, HEAD_DIM), F32),
                   jax.ShapeDtypeStruct((n * N_HEADS, HEAD_DIM), F32)]
                  + [jax.ShapeDtypeStruct((n, w), F32) for w in widths],
        compiler_params=_params(1),
        name="proj",
    )(x, g_pre, w_in, g_gv)


def _bias_tile_kernel(rb_ref, o_ref, *, tile, num_buckets):
    h = pl.program_id(0)
    off = pl.program_id(1)
    key = lax.broadcasted_iota(jnp.int32, (tile, tile), 0)
    qry = lax.broadcasted_iota(jnp.int32, (tile, tile), 1)
    dist = off * tile + qry - key
    bucket = _t5_bucket(dist, num_buckets)
    bias = jnp.zeros((tile, tile), F32)
    for j in range(num_buckets):
        bias = jnp.where(bucket == j, rb_ref[j, h], bias)
    o_ref[...] = jnp.where(dist >= 0, bias * LOG2E, NEG_INF)


def _bias_tiles(rel_bias, tile):
    nb, nh = rel_bias.shape
    return pl.pallas_call(
        functools.partial(_bias_tile_kernel, tile=tile, num_buckets=nb),
        grid=(nh, 3),
        in_specs=[pl.BlockSpec(memory_space=pltpu.SMEM)],
        out_specs=pl.BlockSpec((None, None, tile, tile), lambda h, o: (h, o, 0, 0)),
        out_shape=jax.ShapeDtypeStruct((nh, 3, tile, tile), F32),
        compiler_params=_params(2),
        name="bias_tiles",
    )(rel_bias)


def _lam(lam_ref, lam_init):
    lp = lam_ref[...]
    s1 = jnp.sum(lp[0:1] * lp[1:2], axis=-1, keepdims=True)
    s2 = jnp.sum(lp[2:3] * lp[3:4], axis=-1, keepdims=True)
    return jnp.exp(s1) - jnp.exp(s2) + lam_init


def _attn_kernel(q_ref, k_ref, vt_ref, bias_ref, lam_ref, gsub_ref, o_ref,
                 qz_ref, st_ref, p_ref, alpha_ref, m_ref, acc_ref, *, tile, lam_init):
    i = pl.program_id(2)

    q = q_ref[...]
    lane = lax.broadcasted_iota(jnp.int32, q.shape, 1)
    qz_ref[0] = jnp.where(lane < D_HALF, q, jnp.zeros_like(q))
    qz_ref[1] = jnp.where(lane >= D_HALF, q, jnp.zeros_like(q))
    m_ref[...] = jnp.full(m_ref.shape, NEG_INF, F32)
    acc_ref[...] = jnp.zeros(acc_ref.shape, F32)

    def scores(j, mp):
        kb = k_ref[pl.ds(pl.multiple_of(j * tile, tile), tile), :]
        st_ref[mp] = lax.dot_general(kb, qz_ref[mp], _NT, preferred_element_type=F32)

    def softmax(j, mp):
        st = st_ref[mp] + bias_ref[jnp.minimum(i - j, 2)]
        m_prev = m_ref[mp]
        m_new = jnp.maximum(m_prev, jnp.max(st, axis=0, keepdims=True))
        alpha_ref[mp] = jnp.exp2(m_prev - m_new)
        p_ref[mp] = jnp.exp2(st - m_new).astype(BF16)
        m_ref[mp] = m_new

    def weigh(j, mp):
        acc_ref[mp] = alpha_ref[mp] * acc_ref[mp] + jnp.dot(vt_ref[j], p_ref[mp],
                                                            preferred_element_type=F32)

    scores(0, 0)
    scores(0, 1)
    softmax(0, 0)

    def body(j, carry):
        scores(j + 1, 0)
        weigh(j, 0)
        softmax(j, 1)
        scores(j + 1, 1)
        softmax(j + 1, 0)
        weigh(j, 1)
        return carry

    lax.fori_loop(0, i, body, 0)
    softmax(i, 1)
    weigh(i, 0)
    weigh(i, 1)

    lam = _lam(lam_ref, lam_init)
    a1, a2 = acc_ref[0], acc_ref[1]
    ot = (a1[:HEAD_DIM] / a1[HEAD_DIM:HEAD_DIM + 1]
          - lam * (a2[:HEAD_DIM] / a2[HEAD_DIM:HEAD_DIM + 1]))
    at = ot * lax.rsqrt(jnp.mean(ot * ot, axis=0, keepdims=True) + EPS) * gsub_ref[...]
    o_ref[...] = (at * (1.0 - lam_init)).T.astype(o_ref.dtype)


def _attn_prompt(qk, vt, bias_tiles, lam_params, g_subln, batch, seq, lam_init):
    tile = bias_tiles.shape[-1]
    assert tile >= MAX_DISTANCE and seq % tile == 0 and vt.shape[-1] == tile
    nq = seq // tile
    kernel = functools.partial(_attn_kernel, tile=tile, lam_init=lam_init)
    return pl.pallas_call(
        kernel,
        grid=(batch, N_HEADS, nq),
        in_specs=[
            pl.BlockSpec((tile, HEAD_DIM), lambda b, h, i: (b * nq + i, h)),
            pl.BlockSpec((seq, HEAD_DIM), lambda b, h, i: (b, N_HEADS + h)),
            pl.BlockSpec((nq, VT_ROWS, tile), lambda b, h, i: (b, h, 0)),
            pl.BlockSpec((None, 3, tile, tile), lambda b, h, i: (h, 0, 0, 0)),
            pl.BlockSpec((4, D_HALF), lambda b, h, i: (0, 0)),
            pl.BlockSpec((HEAD_DIM, 1), lambda b, h, i: (0, 0)),
        ],
        out_specs=pl.BlockSpec((tile, HEAD_DIM), lambda b, h, i: (b * nq + i, h)),
        out_shape=jax.ShapeDtypeStruct((batch * seq, ATTN_W), BF16),
        scratch_shapes=[
            pltpu.VMEM((2, tile, HEAD_DIM), BF16),
            pltpu.VMEM((2, tile, tile), F32),
            pltpu.VMEM((2, tile, tile), BF16),
            pltpu.VMEM((2, 1, tile), F32),
            pltpu.VMEM((2, 1, tile), F32),
            pltpu.VMEM((2, VT_ROWS, tile), F32),
        ],
        compiler_params=_params(3),
        name="attn_prompt",
    )(qk, qk, vt, bias_tiles, lam_params, g_subln.reshape(HEAD_DIM, 1))


def _sattn_kernel(pt_ref, q_ref, kn_ref, vn_ref, lam_ref, gsub_ref, rb_ref, *rest,
                  page, t_new, lam_init, num_buckets):
    del pt_ref
    gp = PAGES_PER_STEP
    k_refs, v_refs = rest[:gp], rest[gp:2 * gp]
    o_ref, qs_ref, kbuf, vbuf, bias_ref, m_ref, l_ref, acc_ref = rest[2 * gp:]
    g = pl.program_id(1)
    n_steps = pl.num_programs(1)
    head_rows = 2 * SUBLANES
    rows = N_HEADS * head_rows
    pcols = page * N_HEADS
    cols = gp * pcols

    r = lax.broadcasted_iota(jnp.int32, (rows, 1), 0)
    hrow = r // head_rows
    trow = r % SUBLANES

    def bias_column(bucket):
        col = jnp.full((rows, 1), rb_ref[bucket, 0], F32)
        for hh in range(1, N_HEADS):
            col = jnp.where(hrow == hh, rb_ref[bucket, hh], col)
        return col * LOG2E

    @pl.when(g == 0)
    def _():
        q8 = q_ref[...]
        lane = lax.broadcasted_iota(jnp.int32, (SUBLANES, HEAD_DIM), 1)
        for hh in range(N_HEADS):
            qh = q8[:, hh * HEAD_DIM:(hh + 1) * HEAD_DIM]
            qs_ref[hh * head_rows:hh * head_rows + SUBLANES, :] = jnp.where(lane < D_HALF, qh, 0.0)
            qs_ref[hh * head_rows + SUBLANES:(hh + 1) * head_rows, :] = jnp.where(lane >= D_HALF, qh, 0.0)
        m_ref[...] = jnp.full(m_ref.shape, NEG_INF, F32)
        l_ref[...] = jnp.zeros(l_ref.shape, F32)
        acc_ref[...] = jnp.zeros(acc_ref.shape, F32)
        chead = lax.broadcasted_iota(jnp.int32, (rows, cols), 1) % N_HEADS
        bias_ref[...] = jnp.where(chead == hrow, bias_column(num_buckets - 1), NEG_INF)

    @pl.when(g == n_steps - 1)
    def _():
        c = lax.broadcasted_iota(jnp.int32, (rows, pcols), 1)
        dist = trow + (page - c // N_HEADS)
        bucket = _t5_bucket(dist, num_buckets)
        b = jnp.zeros(dist.shape, F32)
        for j in range(num_buckets):
            b = jnp.where(bucket == j, bias_column(j), b)
        ok = (c % N_HEADS == hrow) & (dist >= 0)
        bias_ref[:, cols - pcols:] = jnp.where(ok, b, NEG_INF)

    qs = qs_ref[...]
    s = [lax.dot_general(qs, k_refs[j][...], _NT, preferred_element_type=F32)
         + bias_ref[:, j * pcols:(j + 1) * pcols] for j in range(gp)]
    m_prev = m_ref[...]
    m_new = m_prev
    for sj in s:
        m_new = jnp.maximum(m_new, jnp.max(sj, axis=-1, keepdims=True))
    alpha = jnp.exp2(m_prev - m_new)
    l_new = alpha * l_ref[...]
    acc = alpha * acc_ref[...]
    for j in range(gp):
        p = jnp.exp2(s[j] - m_new)
        l_new = l_new + jnp.sum(p, axis=-1, keepdims=True)
        acc = acc + jnp.dot(p, v_refs[j][...], preferred_element_type=F32)
    l_ref[...] = l_new
    acc_ref[...] = acc
    m_ref[...] = m_new

    @pl.when(g == n_steps - 1)
    def _():
        lane = lax.broadcasted_iota(jnp.int32, (head_rows, LANES), 1)
        dist = lax.broadcasted_iota(jnp.int32, (head_rows, 1), 0) % SUBLANES - lane
        valid = (dist >= 0) & (lane < t_new)
        bucket = _t5_bucket(dist, num_buckets)
        lam = _lam(lam_ref, lam_init)
        for hh in range(N_HEADS):
            rws = slice(hh * head_rows, (hh + 1) * head_rows)
            b = jnp.zeros((head_rows, LANES), F32)
            for j in range(min(num_buckets, MAX_EXACT)):
                b = jnp.where(bucket == j, rb_ref[j, hh] * LOG2E, b)
            qh = qs_ref[rws, :]
            s_new = jnp.zeros((head_rows, LANES), F32)
            for t in range(t_new):
                krow = kn_ref[t * N_HEADS + hh:t * N_HEADS + hh + 1, :]
                s_new = jnp.where(lane == t, jnp.sum(qh * krow, axis=-1, keepdims=True), s_new)
            s_new = jnp.where(valid, s_new + b, NEG_INF)
            m_prev = m_ref[rws, :]
            m_fin = jnp.maximum(m_prev, jnp.max(s_new, axis=-1, keepdims=True))
            alpha = jnp.exp2(m_prev - m_fin)
            p_new = jnp.exp2(s_new - m_fin)
            l_fin = alpha * l_ref[rws, :] + jnp.sum(p_new, axis=-1, keepdims=True)
            acc = alpha * acc_ref[rws, :]
            for t in range(t_new):
                pt = jnp.sum(jnp.where(lane == t, p_new, 0.0), axis=-1, keepdims=True)
                acc = acc + pt * vn_ref[t * N_HEADS + hh:t * N_HEADS + hh + 1, :]
            o = acc / l_fin
            diff = o[:SUBLANES] - lam * o[SUBLANES:]
            o_ref[:, hh * HEAD_DIM:(hh + 1) * HEAD_DIM] = _rms(diff, gsub_ref[...]) * (1.0 - lam_init)


def _attn_sample(q8, kn, vn, cache_k, cache_v, page_table, layer, rel_bias, lam_params, g_subln,
                 t_new, lam_init):
    nseq = q8.shape[0]
    depth, n_pool, page = cache_k.shape[:3]
    n_pages = page_table.shape[1]
    gp = PAGES_PER_STEP
    assert n_pages % gp == 0 and t_new <= SUBLANES
    assert page >= MAX_DISTANCE and rel_bias.shape[0] >= MAX_EXACT
    ck = cache_k.reshape(depth * n_pool, page * N_HEADS, HEAD_DIM)
    cv = cache_v.reshape(depth * n_pool, page * N_HEADS, HEAD_DIM)
    base = layer * n_pool
    rows = N_HEADS * 2 * SUBLANES
    cols = gp * page * N_HEADS
    nb = rel_bias.shape[0]

    def page_spec(j):
        return pl.BlockSpec((None, page * N_HEADS, HEAD_DIM),
                            lambda b, g, pt: (base + pt[b * n_pages + g * gp + j], 0, 0))

    seq_spec = pl.BlockSpec((None, SUBLANES, ATTN_W), lambda b, g, pt: (b, 0, 0))
    new_spec = pl.BlockSpec((None, SUBLANES * N_HEADS, HEAD_DIM), lambda b, g, pt: (b, 0, 0))
    kernel = functools.partial(_sattn_kernel, page=page, t_new=t_new, lam_init=lam_init, num_buckets=nb)
    grid_spec = pltpu.PrefetchScalarGridSpec(
        num_scalar_prefetch=1,
        grid=(nseq, n_pages // gp),
        in_specs=[seq_spec, new_spec, new_spec,
                  pl.BlockSpec((4, D_HALF), lambda b, g, pt: (0, 0)),
                  pl.BlockSpec((1, HEAD_DIM), lambda b, g, pt: (0, 0)),
                  pl.BlockSpec(memory_space=pltpu.SMEM)]
                 + [page_spec(j) for j in range(gp)] + [page_spec(j) for j in range(gp)],
        out_specs=seq_spec,
        scratch_shapes=[
            pltpu.VMEM((rows, HEAD_DIM), F32),
            pltpu.VMEM((cols, HEAD_DIM), BF16),
            pltpu.VMEM((cols, HEAD_DIM), BF16),
            pltpu.VMEM((rows, cols), F32),
            pltpu.VMEM((rows, 1), F32),
            pltpu.VMEM((rows, 1), F32),
            pltpu.VMEM((rows, HEAD_DIM), F32),
        ],
    )
    return pl.pallas_call(
        kernel,
        grid_spec=grid_spec,
        out_shape=jax.ShapeDtypeStruct((nseq, SUBLANES, ATTN_W), F32),
        compiler_params=_params(2),
        name="attn_sample",
    )(page_table.reshape(-1), q8, kn, vn, lam_params, g_subln, rel_bias,
      *([ck] * gp), *([cv] * gp))


def _mixer_kernel(a_ref, u_ref, gvn_ref, ga_ref, gb_ref, x_ref, ws_ref, bs_ref, wa_ref, wb_ref,
                  wo_ref, gpost_ref, o_ref, b_scr):
    tm = x_ref.shape[0]
    r = lax.broadcasted_iota(jnp.int32, (CHUNK, CHUNK), 0)
    c = lax.broadcasted_iota(jnp.int32, (CHUNK, CHUNK), 1)
    causal = r >= c
    for grp in range(N_GROUPS):
        cols = slice(grp * CHUNK, (grp + 1) * CHUNK)
        wg = ws_ref[grp]
        wg = jnp.where(causal, wg, jnp.zeros_like(wg))
        for ch in range(tm // CHUNK):
            rws = slice(ch * CHUNK, (ch + 1) * CHUNK)
            sp = jnp.dot(wg, gvn_ref[rws, cols].astype(BF16), preferred_element_type=F32)
            sp = sp + bs_ref[:, cols]
            b_scr[rws, cols] = (u_ref[rws, cols] * sp).astype(BF16)
    ya = jnp.dot(a_ref[...], wa_ref[...], preferred_element_type=F32)
    yb = jnp.dot(b_scr[...], wb_ref[...], preferred_element_type=F32)
    mix = jax.nn.sigmoid(ga_ref[...]) * ya + jax.nn.sigmoid(gb_ref[...]) * yb
    mo = jnp.dot(mix.astype(BF16), wo_ref[...], preferred_element_type=F32)
    o_ref[...] = x_ref[...] + _rms(mo, gpost_ref[...])


def _mixer(a, u, gvn, ga, gb, x, ws, bs_tile, w_a, w_b, w_out, g_post, tm):
    n, d = x.shape
    row = lambda w: pl.BlockSpec((tm, w), lambda i: (i, 0))
    return pl.pallas_call(
        _mixer_kernel,
        grid=(n // tm,),
        in_specs=[row(ATTN_W), row(GMLP_W), row(GMLP_W), row(d), row(d), row(d),
                  _resident(ws.shape), _resident(bs_tile.shape), _resident(w_a.shape),
                  _resident(w_b.shape), _resident(w_out.shape), _resident((1, d))],
        out_specs=row(d),
        out_shape=jax.ShapeDtypeStruct((n, d), F32),
        scratch_shapes=[pltpu.VMEM((tm, GMLP_W), BF16)],
        compiler_params=_params(1),
        name="mixer",
    )(a, u, gvn, ga, gb, x, ws, bs_tile, w_a, w_b, w_out, g_post)


def _ffn_kernel(x_ref, gpre_ref, w1_ref, w2_ref, gpost_ref, o_ref):
    d_ff = w2_ref.shape[0]
    x = x_ref[...]
    h = _rms(x, gpre_ref[...]).astype(BF16)
    acc = jnp.zeros(x.shape, F32)
    for c0 in range(0, d_ff, FF_TILE):
        gate = jnp.dot(h, w1_ref[:, c0:c0 + FF_TILE], preferred_element_type=F32)
        up = jnp.dot(h, w1_ref[:, d_ff + c0:d_ff + c0 + FF_TILE], preferred_element_type=F32)
        act = (gate * jax.nn.sigmoid(gate) * up).astype(BF16)
        acc = acc + jnp.dot(act, w2_ref[c0:c0 + FF_TILE, :], preferred_element_type=F32)
    o_ref[...] = x + _rms(acc, gpost_ref[...])


def _ffn(x, g_pre, w1, w2, g_post, tm):
    n, d = x.shape
    assert w2.shape[0] % FF_TILE == 0
    row = pl.BlockSpec((tm, d), lambda i: (i, 0))
    return pl.pallas_call(
        _ffn_kernel,
        grid=(n // tm,),
        in_specs=[row, _resident((1, d)), _resident(w1.shape), _resident(w2.shape), _resident((1, d))],
        out_specs=row,
        out_shape=jax.ShapeDtypeStruct((n, d), F32),
        compiler_params=_params(1),
        name="ffn",
    )(x, g_pre, w1, w2, g_post)


def kernel(x_prompt, x_sample, cache_k, cache_v, page_table, rel_bias, g_mix_pre, w_in, lam_q1, lam_k1, lam_q2, lam_k2, g_subln, g_gmlp_v, w_spatial, b_spatial, w_branch_a, w_branch_b, w_out, g_mix_post, g_ffn_pre, w_ffn_in, w_ffn_out, g_ffn_post):
    batch, seq, d = x_prompt.shape
    nseq, t_new, _ = x_sample.shape
    depth = w_in.shape[0]
    n_s = nseq * t_new
    assert n_s % CHUNK == 0 and CHUNK % t_new == 0
    attn_tile = min(ATTN_TILE, seq)
    tm_p = attn_tile
    tm_s = CHUNK

    bias_tiles = _bias_tiles(rel_bias, attn_tile)
    xp = x_prompt.reshape(batch * seq, d)
    xs = x_sample.reshape(n_s, d)
    row = lambda v: v.reshape(1, -1)
    eye = jnp.eye(CHUNK // t_new, dtype=F32)

    outs = {name: [] for name in ("kp", "vp", "ks", "vs", "gvs")}
    for l in range(depth):
        lam_init = 0.8 - 0.6 * math.exp(-0.3 * l)
        lam_params = jnp.stack([lam_q1[l], lam_k1[l], lam_q2[l], lam_k2[l]])
        w_in_l = w_in[l].astype(BF16)
        w_a, w_b, w_o = (w_branch_a[l].astype(BF16), w_branch_b[l].astype(BF16), w_out[l].astype(BF16))
        w1, w2 = w_ffn_in[l].astype(BF16), w_ffn_out[l].astype(BF16)
        gsub = row(g_subln[l])
        ws_p = w_spatial[l].astype(BF16)
        bs_p = jnp.repeat(b_spatial[l].T, CHUNK, axis=1)
        ws_s = jax.vmap(lambda w: jnp.kron(eye, w[:t_new, :t_new]))(w_spatial[l]).astype(BF16)
        bs_s = jnp.repeat(jnp.tile(b_spatial[l][:, :t_new], (1, CHUNK // t_new)).T, CHUNK, axis=1)

        qk, vt, kp, vp, u, gvn, ga, gb = _proj(xp, row(g_mix_pre[l]), w_in_l, row(g_gmlp_v[l]), tm_p)
        a = _attn_prompt(qk, vt, bias_tiles, lam_params, gsub, batch, seq, lam_init)
        xp = _mixer(a, u, gvn, ga, gb, xp, ws_p, bs_p, w_a, w_b, w_o, row(g_mix_post[l]), tm_p)
        xp = _ffn(xp, row(g_ffn_pre[l]), w1, w2, row(g_ffn_post[l]), tm_p)
        outs["kp"].append(kp.reshape(batch, seq, N_HEADS, HEAD_DIM))
        outs["vp"].append(vp.reshape(batch, seq, N_HEADS, HEAD_DIM))

        qk, _, ks, vs, u, gvn, ga, gb = _proj(xs, row(g_mix_pre[l]), w_in_l, row(g_gmlp_v[l]), tm_s)
        pad_tokens = lambda t: jnp.pad(t, ((0, 0), (0, SUBLANES - t_new)) + ((0, 0),) * (t.ndim - 2))
        q8 = pad_tokens(qk[:, :ATTN_W].astype(F32).reshape(nseq, t_new, ATTN_W))
        kn = pad_tokens(ks.reshape(nseq, t_new, ATTN_W)).reshape(nseq, SUBLANES * N_HEADS, HEAD_DIM)
        vn = pad_tokens(vs.reshape(nseq, t_new, ATTN_W)).reshape(nseq, SUBLANES * N_HEADS, HEAD_DIM)
        a8 = _attn_sample(q8, kn, vn, cache_k, cache_v, page_table, l, rel_bias, lam_params, gsub,
                          t_new, lam_init)
        a = a8[:, :t_new].reshape(n_s, ATTN_W).astype(BF16)
        xs = _mixer(a, u, gvn, ga, gb, xs, ws_s, bs_s, w_a, w_b, w_o, row(g_mix_post[l]), tm_s)
        xs = _ffn(xs, row(g_ffn_pre[l]), w1, w2, row(g_ffn_post[l]), tm_s)
        outs["ks"].append(ks.reshape(nseq, t_new, N_HEADS, HEAD_DIM))
        outs["vs"].append(vs.reshape(nseq, t_new, N_HEADS, HEAD_DIM))
        outs["gvs"].append(gvn.reshape(nseq, t_new, GMLP_W))

    return (xp.reshape(batch, seq, d), xs.reshape(nseq, t_new, d),
            jnp.stack(outs["kp"]), jnp.stack(outs["vp"]),
            jnp.stack(outs["ks"]), jnp.stack(outs["vs"]), jnp.stack(outs["gvs"]))
```

```python
import functools
import math

import jax
import jax.numpy as jnp
from jax import lax
from jax.experimental import pallas as pl
from jax.experimental.pallas import tpu as pltpu

F32 = jnp.float32
BF16 = jnp.bfloat16

LANES = 128
SUBLANES = 8
BF16_ROWS = 16
VMEM_LIMIT_BYTES = 56 * 1024 * 1024

N_HEADS = 4
D_HALF = 64
HEAD_DIM = 2 * D_HALF
ATTN_W = N_HEADS * HEAD_DIM
N_GROUPS = 4
CHUNK = 128
GMLP_W = N_GROUPS * CHUNK
SCALE = D_HALF ** -0.5
MAX_EXACT = 16
MAX_DISTANCE = 128
EPS = 1e-6
NEG_INF = -1e30
LOG2E = math.log2(math.e)
VT_ROWS = HEAD_DIM + BF16_ROWS

ATTN_TILE = 512
PAGES_PER_STEP = 8
FF_TILE = 256

_NT = (((1,), (1,)), ((), ()))


def _rms(x, g):
    return x * lax.rsqrt(jnp.mean(x * x, axis=-1, keepdims=True) + EPS) * g


def _t5_bucket(dist, num_buckets):
    n = jnp.maximum(dist, 0)
    nf = jnp.maximum(n, 1).astype(F32)
    large = MAX_EXACT + (jnp.log(nf / MAX_EXACT) / math.log(MAX_DISTANCE / MAX_EXACT)
                         * (num_buckets - MAX_EXACT)).astype(jnp.int32)
    return jnp.where(n < MAX_EXACT, n, jnp.minimum(large, num_buckets - 1))


def _resident(shape):
    return pl.BlockSpec(shape, lambda *_: (0,) * len(shape), pipeline_mode=pl.Buffered(1))


def _params(n_axes):
    return pltpu.CompilerParams(dimension_semantics=("arbitrary",) * n_axes,
                                vmem_limit_bytes=VMEM_LIMIT_BYTES)


def _proj_kernel(x_ref, g_ref, w_ref, ggv_ref, qk_ref, vt_ref, k_ref, v_ref, u_ref, gvn_ref, ga_ref, gb_ref):
    tm, d_model = x_ref.shape
    h = _rms(x_ref[...], g_ref[...]).astype(BF16)

    def seg(start, width):
        return jnp.dot(h, w_ref[:, start:start + width], preferred_element_type=F32)

    q = seg(0, ATTN_W)
    k = seg(ATTN_W, ATTN_W)
    v = seg(2 * ATTN_W, ATTN_W)
    qk_ref[:, 0:ATTN_W] = (q * (SCALE * LOG2E)).astype(BF16)
    qk_ref[:, ATTN_W:2 * ATTN_W] = k.astype(BF16)
    vt = v.T.astype(BF16)
    pad_row = lax.broadcasted_iota(jnp.int32, (VT_ROWS - HEAD_DIM, tm), 0)
    ones_rows = jnp.where(pad_row == 0, 1.0, 0.0).astype(BF16)
    for hh in range(N_HEADS):
        vt_ref[hh * VT_ROWS:hh * VT_ROWS + HEAD_DIM, :] = vt[hh * HEAD_DIM:(hh + 1) * HEAD_DIM, :]
        vt_ref[hh * VT_ROWS + HEAD_DIM:(hh + 1) * VT_ROWS, :] = ones_rows
        k_ref[pl.ds(hh, tm, stride=N_HEADS), :] = k[:, hh * HEAD_DIM:(hh + 1) * HEAD_DIM]
        v_ref[pl.ds(hh, tm, stride=N_HEADS), :] = v[:, hh * HEAD_DIM:(hh + 1) * HEAD_DIM]
    u_ref[...] = seg(3 * ATTN_W, GMLP_W)
    gvn_ref[...] = _rms(seg(3 * ATTN_W + GMLP_W, GMLP_W), ggv_ref[...])
    ga_ref[...] = seg(3 * ATTN_W + 2 * GMLP_W, d_model)
    gb_ref[...] = seg(3 * ATTN_W + 2 * GMLP_W + d_model, d_model)


def _proj(x, g_pre, w_in, g_gv, tm):
    n, d = x.shape
    in_w = w_in.shape[1]
    row = lambda w: pl.BlockSpec((tm, w), lambda i: (i, 0))
    head_rows = pl.BlockSpec((tm * N_HEADS, HEAD_DIM), lambda i: (i, 0))
    widths = (GMLP_W, GMLP_W, d, d)
    return pl.pallas_call(
        _proj_kernel,
        grid=(n // tm,),
        in_specs=[row(d), _resident((1, d)), _resident((d, in_w)), _resident((1, GMLP_W))],
        out_specs=[row(2 * ATTN_W), pl.BlockSpec((None, N_HEADS * VT_ROWS, tm), lambda i: (i, 0, 0)),
                   head_rows, head_rows] + [row(w) for w in widths],
        out_shape=[jax.ShapeDtypeStruct((n, 2 * ATTN_W), BF16),
                   jax.ShapeDtypeStruct((n // tm, N_HEADS * VT_ROWS, tm), BF16),
                   jax.ShapeDtypeStruct((n * N_HEADS, HEAD_DIM), F32),
                   jax.ShapeDtypeStruct((n * N_HEADS, HEAD_DIM), F32)]
                  + [jax.ShapeDtypeStruct((n, w), F32) for w in widths],
        compiler_params=_params(1),
        name="proj",
    )(x, g_pre, w_in, g_gv)


def _bias_tile_kernel(rb_ref, o_ref, *, tile, num_buckets):
    h = pl.program_id(0)
    off = pl.program_id(1)
    key = lax.broadcasted_iota(jnp.int32, (tile, tile), 0)
    qry = lax.broadcasted_iota(jnp.int32, (tile, tile), 1)
    dist = off * tile + qry - key
    bucket = _t5_bucket(dist, num_buckets)
    bias = jnp.zeros((tile, tile), F32)
    for j in range(num_buckets):
        bias = jnp.where(bucket == j, rb_ref[j, h], bias)
    o_ref[...] = jnp.where(dist >= 0, bias * LOG2E, NEG_INF)


def _bias_tiles(rel_bias, tile):
    nb, nh = rel_bias.shape
    return pl.pallas_call(
        functools.partial(_bias_tile_kernel, tile=tile, num_buckets=nb),
        grid=(nh, 3),
        in_specs=[pl.BlockSpec(memory_space=pltpu.SMEM)],
        out_specs=pl.BlockSpec((None, None, tile, tile), lambda h, o: (h, o, 0, 0)),
        out_shape=jax.ShapeDtypeStruct((nh, 3, tile, tile), F32),
        compiler_params=_params(2),
        name="bias_tiles",
    )(rel_bias)


def _lam(lam_ref, lam_init):
    lp = lam_ref[...]
    s1 = jnp.sum(lp[0:1] * lp[1:2], axis=-1, keepdims=True)
    s2 = jnp.sum(lp[2:3] * lp[3:4], axis=-1, keepdims=True)
    return jnp.exp(s1) - jnp.exp(s2) + lam_init


def _attn_kernel(q_ref, k_ref, vt_ref, bias_ref, rb_ref, lam_ref, gsub_ref, o_ref,
                 qz_ref, st_ref, cmax_ref, shift_ref, p_ref, alpha_ref, m_ref, acc_ref,
                 *, tile, lam_init, far_bucket):
    i = pl.program_id(2)

    q = q_ref[...]
    lane = lax.broadcasted_iota(jnp.int32, q.shape, 1)
    qz_ref[0] = jnp.where(lane < D_HALF, q, jnp.zeros_like(q))
    qz_ref[1] = jnp.where(lane >= D_HALF, q, jnp.zeros_like(q))
    m_ref[...] = jnp.full(m_ref.shape, NEG_INF, F32)
    acc_ref[...] = jnp.zeros(acc_ref.shape, F32)

    far_bias = rb_ref[far_bucket, pl.program_id(1)] * LOG2E

    def scores(j, mp, far):
        kb = k_ref[pl.ds(pl.multiple_of(j * tile, tile), tile), :]
        st = lax.dot_general(kb, qz_ref[mp], _NT, preferred_element_type=F32)
        if far:
            shift = jnp.full((1, tile), far_bias, F32)
        else:
            st = st + bias_ref[jnp.minimum(i - j, 2)]
            shift = jnp.zeros((1, tile), F32)
        st_ref[mp] = st
        cmax_ref[mp] = jnp.max(st, axis=0, keepdims=True) + shift
        shift_ref[mp] = shift

    def softmax(mp):
        m_prev = m_ref[mp]
        m_new = jnp.maximum(m_prev, cmax_ref[mp])
        alpha_ref[mp] = jnp.exp2(m_prev - m_new)
        p_ref[mp] = jnp.exp2(st_ref[mp] - (m_new - shift_ref[mp])).astype(BF16)
        m_ref[mp] = m_new

    def weigh(j, mp):
        acc_ref[mp] = alpha_ref[mp] * acc_ref[mp] + jnp.dot(vt_ref[j], p_ref[mp],
                                                            preferred_element_type=F32)

    scores(0, 0, False)
    scores(0, 1, False)
    softmax(0)

    def body(j, far):
        scores(j + 1, 0, far)
        weigh(j, 0)
        softmax(1)
        scores(j + 1, 1, far)
        softmax(0)
        weigh(j, 1)

    n_far = jnp.maximum(i - 2, 0)
    lax.fori_loop(0, n_far, lambda j, c: (body(j, True), c)[1], 0)
    lax.fori_loop(n_far, i, lambda j, c: (body(j, False), c)[1], 0)
    softmax(1)
    weigh(i, 0)
    weigh(i, 1)

    lam = _lam(lam_ref, lam_init)
    a1, a2 = acc_ref[0], acc_ref[1]
    ot = (a1[:HEAD_DIM] / a1[HEAD_DIM:HEAD_DIM + 1]
          - lam * (a2[:HEAD_DIM] / a2[HEAD_DIM:HEAD_DIM + 1]))
    at = ot * lax.rsqrt(jnp.mean(ot * ot, axis=0, keepdims=True) + EPS) * gsub_ref[...]
    o_ref[...] = (at * (1.0 - lam_init)).T.astype(o_ref.dtype)


def _attn_prompt(qk, vt, bias_tiles, rel_bias, lam_params, g_subln, batch, seq, lam_init):
    tile = bias_tiles.shape[-1]
    assert tile >= MAX_DISTANCE and seq % tile == 0 and vt.shape[-1] == tile
    nq = seq // tile
    kernel = functools.partial(_attn_kernel, tile=tile, lam_init=lam_init,
                               far_bucket=rel_bias.shape[0] - 1)
    return pl.pallas_call(
        kernel,
        grid=(batch, N_HEADS, nq),
        in_specs=[
            pl.BlockSpec((tile, HEAD_DIM), lambda b, h, i: (b * nq + i, h)),
            pl.BlockSpec((seq, HEAD_DIM), lambda b, h, i: (b, N_HEADS + h)),
            pl.BlockSpec((nq, VT_ROWS, tile), lambda b, h, i: (b, h, 0)),
            pl.BlockSpec((None, 3, tile, tile), lambda b, h, i: (h, 0, 0, 0)),
            pl.BlockSpec(memory_space=pltpu.SMEM),
            pl.BlockSpec((4, D_HALF), lambda b, h, i: (0, 0)),
            pl.BlockSpec((HEAD_DIM, 1), lambda b, h, i: (0, 0)),
        ],
        out_specs=pl.BlockSpec((tile, HEAD_DIM), lambda b, h, i: (b * nq + i, h)),
        out_shape=jax.ShapeDtypeStruct((batch * seq, ATTN_W), BF16),
        scratch_shapes=[
            pltpu.VMEM((2, tile, HEAD_DIM), BF16),
            pltpu.VMEM((2, tile, tile), F32),
            pltpu.VMEM((2, 1, tile), F32),
            pltpu.VMEM((2, 1, tile), F32),
            pltpu.VMEM((2, tile, tile), BF16),
            pltpu.VMEM((2, 1, tile), F32),
            pltpu.VMEM((2, 1, tile), F32),
            pltpu.VMEM((2, VT_ROWS, tile), F32),
        ],
        compiler_params=_params(3),
        name="attn_prompt",
    )(qk, qk, vt, bias_tiles, rel_bias, lam_params, g_subln.reshape(HEAD_DIM, 1))


def _sattn_kernel(pt_ref, q_ref, kn_ref, vn_ref, lam_ref, gsub_ref, rb_ref, *rest,
                  page, t_new, lam_init, num_buckets):
    del pt_ref
    gp = PAGES_PER_STEP
    k_refs, v_refs = rest[:gp], rest[gp:2 * gp]
    o_ref, qs_ref, s_ref, bias_ref, m_ref, l_ref, acc_ref = rest[2 * gp:]
    g = pl.program_id(1)
    n_groups = pl.num_programs(1) - 1
    head_rows = 2 * SUBLANES
    rows = N_HEADS * head_rows
    pcols = page * N_HEADS
    cols = gp * pcols

    r = lax.broadcasted_iota(jnp.int32, (rows, 1), 0)
    hrow = r // head_rows
    trow = r % SUBLANES

    def bias_column(bucket):
        col = jnp.full((rows, 1), rb_ref[bucket, 0], F32)
        for hh in range(1, N_HEADS):
            col = jnp.where(hrow == hh, rb_ref[bucket, hh], col)
        return col * LOG2E

    @pl.when(g == 0)
    def _():
        q8 = q_ref[...]
        lane = lax.broadcasted_iota(jnp.int32, (SUBLANES, HEAD_DIM), 1)
        for hh in range(N_HEADS):
            qh = q8[:, hh * HEAD_DIM:(hh + 1) * HEAD_DIM]
            qs_ref[hh * head_rows:hh * head_rows + SUBLANES, :] = jnp.where(lane < D_HALF, qh, 0.0)
            qs_ref[hh * head_rows + SUBLANES:(hh + 1) * head_rows, :] = jnp.where(lane >= D_HALF, qh, 0.0)
        m_ref[...] = jnp.full(m_ref.shape, NEG_INF, F32)
        l_ref[...] = jnp.zeros(l_ref.shape, F32)
        acc_ref[...] = jnp.zeros(acc_ref.shape, F32)
        chead = lax.broadcasted_iota(jnp.int32, (rows, cols), 1) % N_HEADS
        bias_ref[...] = jnp.where(chead == hrow, bias_column(num_buckets - 1), NEG_INF)
        s_ref[1] = jnp.full(s_ref.shape[1:], -jnp.inf, F32)

    @pl.when(g == n_groups)
    def _():
        c = lax.broadcasted_iota(jnp.int32, (rows, pcols), 1)
        dist = trow + (page - c // N_HEADS)
        bucket = _t5_bucket(dist, num_buckets)
        b = jnp.zeros(dist.shape, F32)
        for j in range(num_buckets):
            b = jnp.where(bucket == j, bias_column(j), b)
        ok = (c % N_HEADS == hrow) & (dist >= 0)
        bias_ref[:, cols - pcols:] = jnp.where(ok, b, NEG_INF)

    def stages(cur, prev):
        qs = qs_ref[...]
        for j in range(gp):
            s_ref[cur, :, j * pcols:(j + 1) * pcols] = lax.dot_general(
                qs, k_refs[j][...], _NT, preferred_element_type=F32)

        s = [s_ref[prev, :, j * pcols:(j + 1) * pcols] + bias_ref[:, j * pcols:(j + 1) * pcols]
             for j in range(gp)]
        m_prev = m_ref[...]
        m_new = m_prev
        for sj in s:
            m_new = jnp.maximum(m_new, jnp.max(sj, axis=-1, keepdims=True))
        alpha = jnp.exp2(m_prev - m_new)
        l_new = alpha * l_ref[...]
        acc = alpha * acc_ref[...]
        for j in range(gp):
            p = jnp.exp2(s[j] - m_new)
            l_new = l_new + jnp.sum(p, axis=-1, keepdims=True)
            acc = acc + jnp.dot(p, v_refs[j][...], preferred_element_type=F32)
        l_ref[...] = l_new
        acc_ref[...] = acc
        m_ref[...] = m_new

    @pl.when(g % 2 == 0)
    def _():
        stages(0, 1)

    @pl.when(g % 2 == 1)
    def _():
        stages(1, 0)

    @pl.when(g == n_groups)
    def _():
        lane = lax.broadcasted_iota(jnp.int32, (head_rows, LANES), 1)
        dist = lax.broadcasted_iota(jnp.int32, (head_rows, 1), 0) % SUBLANES - lane
        valid = (dist >= 0) & (lane < t_new)
        bucket = _t5_bucket(dist, num_buckets)
        lam = _lam(lam_ref, lam_init)
        for hh in range(N_HEADS):
            rws = slice(hh * head_rows, (hh + 1) * head_rows)
            b = jnp.zeros((head_rows, LANES), F32)
            for j in range(min(num_buckets, MAX_EXACT)):
                b = jnp.where(bucket == j, rb_ref[j, hh] * LOG2E, b)
            qh = qs_ref[rws, :]
            s_new = jnp.zeros((head_rows, LANES), F32)
            for t in range(t_new):
                krow = kn_ref[t * N_HEADS + hh:t * N_HEADS + hh + 1, :]
                s_new = jnp.where(lane == t, jnp.sum(qh * krow, axis=-1, keepdims=True), s_new)
            s_new = jnp.where(valid, s_new + b, NEG_INF)
            m_prev = m_ref[rws, :]
            m_fin = jnp.maximum(m_prev, jnp.max(s_new, axis=-1, keepdims=True))
            alpha = jnp.exp2(m_prev - m_fin)
            p_new = jnp.exp2(s_new - m_fin)
            l_fin = alpha * l_ref[rws, :] + jnp.sum(p_new, axis=-1, keepdims=True)
            acc = alpha * acc_ref[rws, :]
            for t in range(t_new):
                pt = jnp.sum(jnp.where(lane == t, p_new, 0.0), axis=-1, keepdims=True)
                acc = acc + pt * vn_ref[t * N_HEADS + hh:t * N_HEADS + hh + 1, :]
            o = acc / l_fin
            diff = o[:SUBLANES] - lam * o[SUBLANES:]
            o_ref[:, hh * HEAD_DIM:(hh + 1) * HEAD_DIM] = _rms(diff, gsub_ref[...]) * (1.0 - lam_init)


def _attn_sample(q8, kn, vn, cache_k, cache_v, page_table, layer, rel_bias, lam_params, g_subln,
                 t_new, lam_init):
    nseq = q8.shape[0]
    depth, n_pool, page = cache_k.shape[:3]
    n_pages = page_table.shape[1]
    gp = PAGES_PER_STEP
    assert n_pages % gp == 0 and t_new <= SUBLANES
    assert page >= MAX_DISTANCE and rel_bias.shape[0] >= MAX_EXACT
    ck = cache_k.reshape(depth * n_pool, page * N_HEADS, HEAD_DIM)
    cv = cache_v.reshape(depth * n_pool, page * N_HEADS, HEAD_DIM)
    base = layer * n_pool
    rows = N_HEADS * 2 * SUBLANES
    cols = gp * page * N_HEADS
    nb = rel_bias.shape[0]

    n_groups = n_pages // gp

    def page_spec(j, lag):
        def index(b, g, pt):
            grp = jnp.clip(g - lag, 0, n_groups - 1)
            return (base + pt[b * n_pages + grp * gp + j], 0, 0)
        return pl.BlockSpec((None, page * N_HEADS, HEAD_DIM), index)

    seq_spec = pl.BlockSpec((None, SUBLANES, ATTN_W), lambda b, g, pt: (b, 0, 0))
    new_spec = pl.BlockSpec((None, SUBLANES * N_HEADS, HEAD_DIM), lambda b, g, pt: (b, 0, 0))
    kernel = functools.partial(_sattn_kernel, page=page, t_new=t_new, lam_init=lam_init, num_buckets=nb)
    grid_spec = pltpu.PrefetchScalarGridSpec(
        num_scalar_prefetch=1,
        grid=(nseq, n_groups + 1),
        in_specs=[seq_spec, new_spec, new_spec,
                  pl.BlockSpec((4, D_HALF), lambda b, g, pt: (0, 0)),
                  pl.BlockSpec((1, HEAD_DIM), lambda b, g, pt: (0, 0)),
                  pl.BlockSpec(memory_space=pltpu.SMEM)]
                 + [page_spec(j, 0) for j in range(gp)] + [page_spec(j, 1) for j in range(gp)],
        out_specs=seq_spec,
        scratch_shapes=[
            pltpu.VMEM((rows, HEAD_DIM), F32),
            pltpu.VMEM((2, rows, cols), F32),
            pltpu.VMEM((rows, cols), F32),
            pltpu.VMEM((rows, 1), F32),
            pltpu.VMEM((rows, 1), F32),
            pltpu.VMEM((rows, HEAD_DIM), F32),
        ],
    )
    return pl.pallas_call(
        kernel,
        grid_spec=grid_spec,
        out_shape=jax.ShapeDtypeStruct((nseq, SUBLANES, ATTN_W), F32),
        compiler_params=_params(2),
        name="attn_sample",
    )(page_table.reshape(-1), q8, kn, vn, lam_params, g_subln, rel_bias,
      *([ck] * gp), *([cv] * gp))


def _mixer_kernel(a_ref, u_ref, gvn_ref, ga_ref, gb_ref, x_ref, ws_ref, bs_ref, wa_ref, wb_ref,
                  wo_ref, gpost_ref, o_ref, b_scr):
    tm = x_ref.shape[0]
    r = lax.broadcasted_iota(jnp.int32, (CHUNK, CHUNK), 0)
    c = lax.broadcasted_iota(jnp.int32, (CHUNK, CHUNK), 1)
    causal = r >= c
    for grp in range(N_GROUPS):
        cols = slice(grp * CHUNK, (grp + 1) * CHUNK)
        wg = ws_ref[grp]
        wg = jnp.where(causal, wg, jnp.zeros_like(wg))
        for ch in range(tm // CHUNK):
            rws = slice(ch * CHUNK, (ch + 1) * CHUNK)
            sp = jnp.dot(wg, gvn_ref[rws, cols].astype(BF16), preferred_element_type=F32)
            sp = sp + bs_ref[:, cols]
            b_scr[rws, cols] = (u_ref[rws, cols] * sp).astype(BF16)
    ya = jnp.dot(a_ref[...], wa_ref[...], preferred_element_type=F32)
    yb = jnp.dot(b_scr[...], wb_ref[...], preferred_element_type=F32)
    mix = jax.nn.sigmoid(ga_ref[...]) * ya + jax.nn.sigmoid(gb_ref[...]) * yb
    mo = jnp.dot(mix.astype(BF16), wo_ref[...], preferred_element_type=F32)
    o_ref[...] = x_ref[...] + _rms(mo, gpost_ref[...])


def _mixer(a, u, gvn, ga, gb, x, ws, bs_tile, w_a, w_b, w_out, g_post, tm):
    n, d = x.shape
    row = lambda w: pl.BlockSpec((tm, w), lambda i: (i, 0))
    return pl.pallas_call(
        _mixer_kernel,
        grid=(n // tm,),
        in_specs=[row(ATTN_W), row(GMLP_W), row(GMLP_W), row(d), row(d), row(d),
                  _resident(ws.shape), _resident(bs_tile.shape), _resident(w_a.shape),
                  _resident(w_b.shape), _resident(w_out.shape), _resident((1, d))],
        out_specs=row(d),
        out_shape=jax.ShapeDtypeStruct((n, d), F32),
        scratch_shapes=[pltpu.VMEM((tm, GMLP_W), BF16)],
        compiler_params=_params(1),
        name="mixer",
    )(a, u, gvn, ga, gb, x, ws, bs_tile, w_a, w_b, w_out, g_post)


def _ffn_kernel(x_ref, gpre_ref, w1_ref, w2_ref, gpost_ref, o_ref):
    d_ff = w2_ref.shape[0]
    x = x_ref[...]
    h = _rms(x, gpre_ref[...]).astype(BF16)
    acc = jnp.zeros(x.shape, F32)
    for c0 in range(0, d_ff, FF_TILE):
        gate = jnp.dot(h, w1_ref[:, c0:c0 + FF_TILE], preferred_element_type=F32)
        up = jnp.dot(h, w1_ref[:, d_ff + c0:d_ff + c0 + FF_TILE], preferred_element_type=F32)
        act = (gate * jax.nn.sigmoid(gate) * up).astype(BF16)
        acc = acc + jnp.dot(act, w2_ref[c0:c0 + FF_TILE, :], preferred_element_type=F32)
    o_ref[...] = x + _rms(acc, gpost_ref[...])


def _ffn(x, g_pre, w1, w2, g_post, tm):
    n, d = x.shape
    assert w2.shape[0] % FF_TILE == 0
    row = pl.BlockSpec((tm, d), lambda i: (i, 0))
    return pl.pallas_call(
        _ffn_kernel,
        grid=(n // tm,),
        in_specs=[row, _resident((1, d)), _resident(w1.shape), _resident(w2.shape), _resident((1, d))],
        out_specs=row,
        out_shape=jax.ShapeDtypeStruct((n, d), F32),
        compiler_params=_params(1),
        name="ffn",
    )(x, g_pre, w1, w2, g_post)


def kernel(x_prompt, x_sample, cache_k, cache_v, page_table, rel_bias, g_mix_pre, w_in, lam_q1, lam_k1, lam_q2, lam_k2, g_subln, g_gmlp_v, w_spatial, b_spatial, w_branch_a, w_branch_b, w_out, g_mix_post, g_ffn_pre, w_ffn_in, w_ffn_out, g_ffn_post):
    batch, seq, d = x_prompt.shape
    nseq, t_new, _ = x_sample.shape
    depth = w_in.shape[0]
    n_s = nseq * t_new
    assert n_s % CHUNK == 0 and CHUNK % t_new == 0
    attn_tile = min(ATTN_TILE, seq)
    tm_p = attn_tile
    tm_s = CHUNK

    bias_tiles = _bias_tiles(rel_bias, attn_tile)
    xp = x_prompt.reshape(batch * seq, d)
    xs = x_sample.reshape(n_s, d)
    row = lambda v: v.reshape(1, -1)
    eye = jnp.eye(CHUNK // t_new, dtype=F32)

    outs = {name: [] for name in ("kp", "vp", "ks", "vs", "gvs")}
    for l in range(depth):
        lam_init = 0.8 - 0.6 * math.exp(-0.3 * l)
        lam_params = jnp.stack([lam_q1[l], lam_k1[l], lam_q2[l], lam_k2[l]])
        w_in_l = w_in[l].astype(BF16)
        w_a, w_b, w_o = (w_branch_a[l].astype(BF16), w_branch_b[l].astype(BF16), w_out[l].astype(BF16))
        w1, w2 = w_ffn_in[l].astype(BF16), w_ffn_out[l].astype(BF16)
        gsub = row(g_subln[l])
        ws_p = w_spatial[l].astype(BF16)
        bs_p = jnp.repeat(b_spatial[l].T, CHUNK, axis=1)
        ws_s = jax.vmap(lambda w: jnp.kron(eye, w[:t_new, :t_new]))(w_spatial[l]).astype(BF16)
        bs_s = jnp.repeat(jnp.tile(b_spatial[l][:, :t_new], (1, CHUNK // t_new)).T, CHUNK, axis=1)

        qk, vt, kp, vp, u, gvn, ga, gb = _proj(xp, row(g_mix_pre[l]), w_in_l, row(g_gmlp_v[l]), tm_p)
        a = _attn_prompt(qk, vt, bias_tiles, rel_bias, lam_params, gsub, batch, seq, lam_init)
        xp = _mixer(a, u, gvn, ga, gb, xp, ws_p, bs_p, w_a, w_b, w_o, row(g_mix_post[l]), tm_p)
        xp = _ffn(xp, row(g_ffn_pre[l]), w1, w2, row(g_ffn_post[l]), tm_p)
        outs["kp"].append(kp.reshape(batch, seq, N_HEADS, HEAD_DIM))
        outs["vp"].append(vp.reshape(batch, seq, N_HEADS, HEAD_DIM))

        qk, _, ks, vs, u, gvn, ga, gb = _proj(xs, row(g_mix_pre[l]), w_in_l, row(g_gmlp_v[l]), tm_s)
        pad_tokens = lambda t: jnp.pad(t, ((0, 0), (0, SUBLANES - t_new)) + ((0, 0),) * (t.ndim - 2))
        q8 = pad_tokens(qk[:, :ATTN_W].astype(F32).reshape(nseq, t_new, ATTN_W))
        kn = pad_tokens(ks.reshape(nseq, t_new, ATTN_W)).reshape(nseq, SUBLANES * N_HEADS, HEAD_DIM)
        vn = pad_tokens(vs.reshape(nseq, t_new, ATTN_W)).reshape(nseq, SUBLANES * N_HEADS, HEAD_DIM)
        a8 = _attn_sample(q8, kn, vn, cache_k, cache_v, page_table, l, rel_bias, lam_params, gsub,
                          t_new, lam_init)
        a = a8[:, :t_new].reshape(n_s, ATTN_W).astype(BF16)
        xs = _mixer(a, u, gvn, ga, gb, xs, ws_s, bs_s, w_a, w_b, w_o, row(g_mix_post[l]), tm_s)
        xs = _ffn(xs, row(g_ffn_pre[l]), w1, w2, row(g_ffn_post[l]), tm_s)
        outs["ks"].append(ks.reshape(nseq, t_new, N_HEADS, HEAD_DIM))
        outs["vs"].append(vs.reshape(nseq, t_new, N_HEADS, HEAD_DIM))
        outs["gvs"].append(gvn.reshape(nseq, t_new, GMLP_W))

    return (xp.reshape(batch, seq, d), xs.reshape(nseq, t_new, d),
            jnp.stack(outs["kp"]), jnp.stack(outs["vp"]),
            jnp.stack(outs["ks"]), jnp.stack(outs["vs"]), jnp.stack(outs["gvs"]))
```

```python
import functools
import math

import jax
import jax.numpy as jnp
from jax import lax
from jax.experimental import pallas as pl
from jax.experimental.pallas import tpu as pltpu

F32 = jnp.float32
BF16 = jnp.bfloat16

LANES = 128
SUBLANES = 8
BF16_ROWS = 16
VMEM_LIMIT_BYTES = 56 * 1024 * 1024

N_HEADS = 4
D_HALF = 64
HEAD_DIM = 2 * D_HALF
ATTN_W = N_HEADS * HEAD_DIM
N_GROUPS = 4
CHUNK = 128
GMLP_W = N_GROUPS * CHUNK
SCALE = D_HALF ** -0.5
MAX_EXACT = 16
MAX_DISTANCE = 128
EPS = 1e-6
NEG_INF = -1e30
LOG2E = math.log2(math.e)
VT_ROWS = HEAD_DIM + BF16_ROWS

ATTN_TILE = 512
PAGES_PER_STEP = 16
FF_TILE = 256

_NT = (((1,), (1,)), ((), ()))


def _rms(x, g):
    return x * lax.rsqrt(jnp.mean(x * x, axis=-1, keepdims=True) + EPS) * g


def _t5_bucket(dist, num_buckets):
    n = jnp.maximum(dist, 0)
    nf = jnp.maximum(n, 1).astype(F32)
    large = MAX_EXACT + (jnp.log(nf / MAX_EXACT) / math.log(MAX_DISTANCE / MAX_EXACT)
                         * (num_buckets - MAX_EXACT)).astype(jnp.int32)
    return jnp.where(n < MAX_EXACT, n, jnp.minimum(large, num_buckets - 1))


def _resident(shape):
    return pl.BlockSpec(shape, lambda *_: (0,) * len(shape), pipeline_mode=pl.Buffered(1))


def _params(n_axes):
    return pltpu.CompilerParams(dimension_semantics=("arbitrary",) * n_axes,
                                vmem_limit_bytes=VMEM_LIMIT_BYTES)


def _proj_kernel(x_ref, g_ref, w_ref, ggv_ref, qk_ref, vt_ref, k_ref, v_ref, u_ref, gvn_ref, ga_ref, gb_ref):
    tm, d_model = x_ref.shape
    h = _rms(x_ref[...], g_ref[...]).astype(BF16)

    def seg(start, width):
        return jnp.dot(h, w_ref[:, start:start + width], preferred_element_type=F32)

    q = seg(0, ATTN_W)
    k = seg(ATTN_W, ATTN_W)
    v = seg(2 * ATTN_W, ATTN_W)
    qk_ref[:, 0:ATTN_W] = (q * (SCALE * LOG2E)).astype(BF16)
    qk_ref[:, ATTN_W:2 * ATTN_W] = k.astype(BF16)
    vt = v.T.astype(BF16)
    pad_row = lax.broadcasted_iota(jnp.int32, (VT_ROWS - HEAD_DIM, tm), 0)
    ones_rows = jnp.where(pad_row == 0, 1.0, 0.0).astype(BF16)
    for hh in range(N_HEADS):
        vt_ref[hh * VT_ROWS:hh * VT_ROWS + HEAD_DIM, :] = vt[hh * HEAD_DIM:(hh + 1) * HEAD_DIM, :]
        vt_ref[hh * VT_ROWS + HEAD_DIM:(hh + 1) * VT_ROWS, :] = ones_rows
        k_ref[pl.ds(hh, tm, stride=N_HEADS), :] = k[:, hh * HEAD_DIM:(hh + 1) * HEAD_DIM]
        v_ref[pl.ds(hh, tm, stride=N_HEADS), :] = v[:, hh * HEAD_DIM:(hh + 1) * HEAD_DIM]
    u_ref[...] = seg(3 * ATTN_W, GMLP_W)
    gvn_ref[...] = _rms(seg(3 * ATTN_W + GMLP_W, GMLP_W), ggv_ref[...])
    ga_ref[...] = seg(3 * ATTN_W + 2 * GMLP_W, d_model)
    gb_ref[...] = seg(3 * ATTN_W + 2 * GMLP_W + d_model, d_model)


def _proj(x, g_pre, w_in, g_gv, tm):
    n, d = x.shape
    in_w = w_in.shape[1]
    row = lambda w: pl.BlockSpec((tm, w), lambda i: (i, 0))
    head_rows = pl.BlockSpec((tm * N_HEADS, HEAD_DIM), lambda i: (i, 0))
    widths = (GMLP_W, GMLP_W, d, d)
    return pl.pallas_call(
        _proj_kernel,
        grid=(n // tm,),
        in_specs=[row(d), _resident((1, d)), _resident((d, in_w)), _resident((1, GMLP_W))],
        out_specs=[row(2 * ATTN_W), pl.BlockSpec((None, N_HEADS * VT_ROWS, tm), lambda i: (i, 0, 0)),
                   head_rows, head_rows] + [row(w) for w in widths],
        out_shape=[jax.ShapeDtypeStruct((n, 2 * ATTN_W), BF16),
                   jax.ShapeDtypeStruct((n // tm, N_HEADS * VT_ROWS, tm), BF16),
                   jax.ShapeDtypeStruct((n * N_HEADS, HEAD_DIM), F32),
                   jax.ShapeDtypeStruct((n * N_HEADS, HEAD_DIM), F32)]
                  + [jax.ShapeDtypeStruct((n, w), F32) for w in widths],
        compiler_params=_params(1),
        name="proj",
    )(x, g_pre, w_in, g_gv)


def _bias_tile_kernel(rb_ref, o_ref, *, tile, num_buckets):
    h = pl.program_id(0)
    off = pl.program_id(1)
    key = lax.broadcasted_iota(jnp.int32, (tile, tile), 0)
    qry = lax.broadcasted_iota(jnp.int32, (tile, tile), 1)
    dist = off * tile + qry - key
    bucket = _t5_bucket(dist, num_buckets)
    bias = jnp.zeros((tile, tile), F32)
    for j in range(num_buckets):
        bias = jnp.where(bucket == j, rb_ref[j, h], bias)
    o_ref[...] = jnp.where(dist >= 0, bias * LOG2E, NEG_INF)


def _bias_tiles(rel_bias, tile):
    nb, nh = rel_bias.shape
    return pl.pallas_call(
        functools.partial(_bias_tile_kernel, tile=tile, num_buckets=nb),
        grid=(nh, 3),
        in_specs=[pl.BlockSpec(memory_space=pltpu.SMEM)],
        out_specs=pl.BlockSpec((None, None, tile, tile), lambda h, o: (h, o, 0, 0)),
        out_shape=jax.ShapeDtypeStruct((nh, 3, tile, tile), F32),
        compiler_params=_params(2),
        name="bias_tiles",
    )(rel_bias)


def _lam(lam_ref, lam_init):
    lp = lam_ref[...]
    s1 = jnp.sum(lp[0:1] * lp[1:2], axis=-1, keepdims=True)
    s2 = jnp.sum(lp[2:3] * lp[3:4], axis=-1, keepdims=True)
    return jnp.exp(s1) - jnp.exp(s2) + lam_init


def _attn_kernel(q_ref, k_ref, vt_ref, bias_ref, rb_ref, lam_ref, gsub_ref, o_ref,
                 qz_ref, st_ref, cmax_ref, shift_ref, p_ref, alpha_ref, m_ref, acc_ref,
                 *, tile, lam_init, far_bucket):
    i = pl.program_id(2)

    q = q_ref[...]
    lane = lax.broadcasted_iota(jnp.int32, q.shape, 1)
    qz_ref[0] = jnp.where(lane < D_HALF, q, jnp.zeros_like(q))
    qz_ref[1] = jnp.where(lane >= D_HALF, q, jnp.zeros_like(q))
    m_ref[...] = jnp.full(m_ref.shape, NEG_INF, F32)
    acc_ref[...] = jnp.zeros(acc_ref.shape, F32)

    far_bias = rb_ref[far_bucket, pl.program_id(1)] * LOG2E

    def scores(j, mp, far):
        kb = k_ref[pl.ds(pl.multiple_of(j * tile, tile), tile), :]
        st = lax.dot_general(kb, qz_ref[mp], _NT, preferred_element_type=F32)
        if far:
            shift = jnp.full((1, tile), far_bias, F32)
        else:
            st = st + bias_ref[jnp.minimum(i - j, 2)]
            shift = jnp.zeros((1, tile), F32)
        st_ref[mp] = st
        cmax_ref[mp] = jnp.max(st, axis=0, keepdims=True) + shift
        shift_ref[mp] = shift

    def softmax(mp):
        m_prev = m_ref[mp]
        m_new = jnp.maximum(m_prev, cmax_ref[mp])
        alpha_ref[mp] = jnp.exp2(m_prev - m_new)
        p_ref[mp] = jnp.exp2(st_ref[mp] - (m_new - shift_ref[mp])).astype(BF16)
        m_ref[mp] = m_new

    def weigh(j, mp):
        acc_ref[mp] = alpha_ref[mp] * acc_ref[mp] + jnp.dot(vt_ref[j], p_ref[mp],
                                                            preferred_element_type=F32)

    scores(0, 0, False)
    scores(0, 1, False)
    softmax(0)

    def body(j, far):
        scores(j + 1, 0, far)
        weigh(j, 0)
        softmax(1)
        scores(j + 1, 1, far)
        softmax(0)
        weigh(j, 1)

    n_far = jnp.maximum(i - 2, 0)
    lax.fori_loop(0, n_far, lambda j, c: (body(j, True), c)[1], 0)
    lax.fori_loop(n_far, i, lambda j, c: (body(j, False), c)[1], 0)
    softmax(1)
    weigh(i, 0)
    weigh(i, 1)

    lam = _lam(lam_ref, lam_init)
    a1, a2 = acc_ref[0], acc_ref[1]
    ot = (a1[:HEAD_DIM] / a1[HEAD_DIM:HEAD_DIM + 1]
          - lam * (a2[:HEAD_DIM] / a2[HEAD_DIM:HEAD_DIM + 1]))
    at = ot * lax.rsqrt(jnp.mean(ot * ot, axis=0, keepdims=True) + EPS) * gsub_ref[...]
    o_ref[...] = (at * (1.0 - lam_init)).T.astype(o_ref.dtype)


def _attn_prompt(qk, vt, bias_tiles, rel_bias, lam_params, g_subln, batch, seq, lam_init):
    tile = bias_tiles.shape[-1]
    assert tile >= MAX_DISTANCE and seq % tile == 0 and vt.shape[-1] == tile
    nq = seq // tile
    kernel = functools.partial(_attn_kernel, tile=tile, lam_init=lam_init,
                               far_bucket=rel_bias.shape[0] - 1)
    return pl.pallas_call(
        kernel,
        grid=(batch, N_HEADS, nq),
        in_specs=[
            pl.BlockSpec((tile, HEAD_DIM), lambda b, h, i: (b * nq + i, h)),
            pl.BlockSpec((seq, HEAD_DIM), lambda b, h, i: (b, N_HEADS + h)),
            pl.BlockSpec((nq, VT_ROWS, tile), lambda b, h, i: (b, h, 0)),
            pl.BlockSpec((None, 3, tile, tile), lambda b, h, i: (h, 0, 0, 0)),
            pl.BlockSpec(memory_space=pltpu.SMEM),
            pl.BlockSpec((4, D_HALF), lambda b, h, i: (0, 0)),
            pl.BlockSpec((HEAD_DIM, 1), lambda b, h, i: (0, 0)),
        ],
        out_specs=pl.BlockSpec((tile, HEAD_DIM), lambda b, h, i: (b * nq + i, h)),
        out_shape=jax.ShapeDtypeStruct((batch * seq, ATTN_W), BF16),
        scratch_shapes=[
            pltpu.VMEM((2, tile, HEAD_DIM), BF16),
            pltpu.VMEM((2, tile, tile), F32),
            pltpu.VMEM((2, 1, tile), F32),
            pltpu.VMEM((2, 1, tile), F32),
            pltpu.VMEM((2, tile, tile), BF16),
            pltpu.VMEM((2, 1, tile), F32),
            pltpu.VMEM((2, 1, tile), F32),
            pltpu.VMEM((2, VT_ROWS, tile), F32),
        ],
        compiler_params=_params(3),
        name="attn_prompt",
    )(qk, qk, vt, bias_tiles, rel_bias, lam_params, g_subln.reshape(HEAD_DIM, 1))


def _sattn_kernel(pt_ref, q_ref, kn_ref, vn_ref, lam_ref, gsub_ref, rb_ref, *rest,
                  gp, page, t_new, lam_init, num_buckets):
    del pt_ref
    k_refs, v_refs = rest[:gp], rest[gp:2 * gp]
    o_ref, qs_ref, s_ref, bias_ref, m_ref, l_ref, acc_ref = rest[2 * gp:]
    g = pl.program_id(1)
    n_groups = pl.num_programs(1) - 1
    head_rows = 2 * SUBLANES
    rows = N_HEADS * head_rows
    pcols = page * N_HEADS
    cols = gp * pcols

    r = lax.broadcasted_iota(jnp.int32, (rows, 1), 0)
    hrow = r // head_rows
    trow = r % SUBLANES

    def bias_column(bucket):
        col = jnp.full((rows, 1), rb_ref[bucket, 0], F32)
        for hh in range(1, N_HEADS):
            col = jnp.where(hrow == hh, rb_ref[bucket, hh], col)
        return col * LOG2E

    @pl.when(g == 0)
    def _():
        q8 = q_ref[...]
        lane = lax.broadcasted_iota(jnp.int32, (SUBLANES, HEAD_DIM), 1)
        for hh in range(N_HEADS):
            qh = q8[:, hh * HEAD_DIM:(hh + 1) * HEAD_DIM]
            qs_ref[hh * head_rows:hh * head_rows + SUBLANES, :] = jnp.where(lane < D_HALF, qh, 0.0)
            qs_ref[hh * head_rows + SUBLANES:(hh + 1) * head_rows, :] = jnp.where(lane >= D_HALF, qh, 0.0)
        m_ref[...] = jnp.full(m_ref.shape, NEG_INF, F32)
        l_ref[...] = jnp.zeros(l_ref.shape, F32)
        acc_ref[...] = jnp.zeros(acc_ref.shape, F32)
        chead = lax.broadcasted_iota(jnp.int32, (rows, cols), 1) % N_HEADS
        bias_ref[...] = jnp.where(chead == hrow, bias_column(num_buckets - 1), NEG_INF)
        s_ref[1] = jnp.full(s_ref.shape[1:], -jnp.inf, F32)

    @pl.when(g == n_groups)
    def _():
        c = lax.broadcasted_iota(jnp.int32, (rows, pcols), 1)
        dist = trow + (page - c // N_HEADS)
        bucket = _t5_bucket(dist, num_buckets)
        b = jnp.zeros(dist.shape, F32)
        for j in range(num_buckets):
            b = jnp.where(bucket == j, bias_column(j), b)
        ok = (c % N_HEADS == hrow) & (dist >= 0)
        bias_ref[:, cols - pcols:] = jnp.where(ok, b, NEG_INF)

    def stages(cur, prev):
        qs = qs_ref[...]
        for j in range(gp):
            s_ref[cur, :, j * pcols:(j + 1) * pcols] = lax.dot_general(
                qs, k_refs[j][...], _NT, preferred_element_type=F32)

        s = [s_ref[prev, :, j * pcols:(j + 1) * pcols] + bias_ref[:, j * pcols:(j + 1) * pcols]
             for j in range(gp)]
        m_prev = m_ref[...]
        m_new = m_prev
        for sj in s:
            m_new = jnp.maximum(m_new, jnp.max(sj, axis=-1, keepdims=True))
        alpha = jnp.exp2(m_prev - m_new)
        l_new = alpha * l_ref[...]
        acc = alpha * acc_ref[...]
        for j in range(gp):
            p = jnp.exp2(s[j] - m_new)
            l_new = l_new + jnp.sum(p, axis=-1, keepdims=True)
            acc = acc + jnp.dot(p, v_refs[j][...], preferred_element_type=F32)
        l_ref[...] = l_new
        acc_ref[...] = acc
        m_ref[...] = m_new

    @pl.when(g % 2 == 0)
    def _():
        stages(0, 1)

    @pl.when(g % 2 == 1)
    def _():
        stages(1, 0)

    @pl.when(g == n_groups)
    def _():
        lane = lax.broadcasted_iota(jnp.int32, (head_rows, LANES), 1)
        dist = lax.broadcasted_iota(jnp.int32, (head_rows, 1), 0) % SUBLANES - lane
        valid = (dist >= 0) & (lane < t_new)
        bucket = _t5_bucket(dist, num_buckets)
        lam = _lam(lam_ref, lam_init)
        for hh in range(N_HEADS):
            rws = slice(hh * head_rows, (hh + 1) * head_rows)
            b = jnp.zeros((head_rows, LANES), F32)
            for j in range(min(num_buckets, MAX_EXACT)):
                b = jnp.where(bucket == j, rb_ref[j, hh] * LOG2E, b)
            qh = qs_ref[rws, :]
            s_new = jnp.zeros((head_rows, LANES), F32)
            for t in range(t_new):
                krow = kn_ref[t * N_HEADS + hh:t * N_HEADS + hh + 1, :]
                s_new = jnp.where(lane == t, jnp.sum(qh * krow, axis=-1, keepdims=True), s_new)
            s_new = jnp.where(valid, s_new + b, NEG_INF)
            m_prev = m_ref[rws, :]
            m_fin = jnp.maximum(m_prev, jnp.max(s_new, axis=-1, keepdims=True))
            alpha = jnp.exp2(m_prev - m_fin)
            p_new = jnp.exp2(s_new - m_fin)
            l_fin = alpha * l_ref[rws, :] + jnp.sum(p_new, axis=-1, keepdims=True)
            acc = alpha * acc_ref[rws, :]
            for t in range(t_new):
                pt = jnp.sum(jnp.where(lane == t, p_new, 0.0), axis=-1, keepdims=True)
                acc = acc + pt * vn_ref[t * N_HEADS + hh:t * N_HEADS + hh + 1, :]
            o = acc / l_fin
            diff = o[:SUBLANES] - lam * o[SUBLANES:]
            o_ref[:, hh * HEAD_DIM:(hh + 1) * HEAD_DIM] = _rms(diff, gsub_ref[...]) * (1.0 - lam_init)


def _attn_sample(q8, kn, vn, cache_k, cache_v, page_table, layer, rel_bias, lam_params, g_subln,
                 t_new, lam_init):
    nseq = q8.shape[0]
    depth, n_pool, page = cache_k.shape[:3]
    n_pages = page_table.shape[1]
    gp = min(PAGES_PER_STEP, n_pages)
    assert n_pages % gp == 0 and t_new <= SUBLANES
    assert page >= MAX_DISTANCE and rel_bias.shape[0] >= MAX_EXACT
    ck = cache_k.reshape(depth * n_pool, page * N_HEADS, HEAD_DIM)
    cv = cache_v.reshape(depth * n_pool, page * N_HEADS, HEAD_DIM)
    base = layer * n_pool
    rows = N_HEADS * 2 * SUBLANES
    cols = gp * page * N_HEADS
    nb = rel_bias.shape[0]

    n_groups = n_pages // gp

    def page_spec(j, lag):
        def index(b, g, pt):
            grp = jnp.clip(g - lag, 0, n_groups - 1)
            return (base + pt[b * n_pages + grp * gp + j], 0, 0)
        return pl.BlockSpec((None, page * N_HEADS, HEAD_DIM), index)

    seq_spec = pl.BlockSpec((None, SUBLANES, ATTN_W), lambda b, g, pt: (b, 0, 0))
    new_spec = pl.BlockSpec((None, SUBLANES * N_HEADS, HEAD_DIM), lambda b, g, pt: (b, 0, 0))
    kernel = functools.partial(_sattn_kernel, gp=gp, page=page, t_new=t_new, lam_init=lam_init,
                               num_buckets=nb)
    grid_spec = pltpu.PrefetchScalarGridSpec(
        num_scalar_prefetch=1,
        grid=(nseq, n_groups + 1),
        in_specs=[seq_spec, new_spec, new_spec,
                  pl.BlockSpec((4, D_HALF), lambda b, g, pt: (0, 0)),
                  pl.BlockSpec((1, HEAD_DIM), lambda b, g, pt: (0, 0)),
                  pl.BlockSpec(memory_space=pltpu.SMEM)]
                 + [page_spec(j, 0) for j in range(gp)] + [page_spec(j, 1) for j in range(gp)],
        out_specs=seq_spec,
        scratch_shapes=[
            pltpu.VMEM((rows, HEAD_DIM), F32),
            pltpu.VMEM((2, rows, cols), F32),
            pltpu.VMEM((rows, cols), F32),
            pltpu.VMEM((rows, 1), F32),
            pltpu.VMEM((rows, 1), F32),
            pltpu.VMEM((rows, HEAD_DIM), F32),
        ],
    )
    return pl.pallas_call(
        kernel,
        grid_spec=grid_spec,
        out_shape=jax.ShapeDtypeStruct((nseq, SUBLANES, ATTN_W), F32),
        compiler_params=_params(2),
        name="attn_sample",
    )(page_table.reshape(-1), q8, kn, vn, lam_params, g_subln, rel_bias,
      *([ck] * gp), *([cv] * gp))


def _mixer_kernel(a_ref, u_ref, gvn_ref, ga_ref, gb_ref, x_ref, ws_ref, bs_ref, wa_ref, wb_ref,
                  wo_ref, gpost_ref, o_ref, b_scr):
    tm = x_ref.shape[0]
    r = lax.broadcasted_iota(jnp.int32, (CHUNK, CHUNK), 0)
    c = lax.broadcasted_iota(jnp.int32, (CHUNK, CHUNK), 1)
    causal = r >= c
    for grp in range(N_GROUPS):
        cols = slice(grp * CHUNK, (grp + 1) * CHUNK)
        wg = ws_ref[grp]
        wg = jnp.where(causal, wg, jnp.zeros_like(wg))
        for ch in range(tm // CHUNK):
            rws = slice(ch * CHUNK, (ch + 1) * CHUNK)
            sp = jnp.dot(wg, gvn_ref[rws, cols].astype(BF16), preferred_element_type=F32)
            sp = sp + bs_ref[:, cols]
            b_scr[rws, cols] = (u_ref[rws, cols] * sp).astype(BF16)
    ya = jnp.dot(a_ref[...], wa_ref[...], preferred_element_type=F32)
    yb = jnp.dot(b_scr[...], wb_ref[...], preferred_element_type=F32)
    mix = jax.nn.sigmoid(ga_ref[...]) * ya + jax.nn.sigmoid(gb_ref[...]) * yb
    mo = jnp.dot(mix.astype(BF16), wo_ref[...], preferred_element_type=F32)
    o_ref[...] = x_ref[...] + _rms(mo, gpost_ref[...])


def _mixer(a, u, gvn, ga, gb, x, ws, bs_tile, w_a, w_b, w_out, g_post, tm):
    n, d = x.shape
    row = lambda w: pl.BlockSpec((tm, w), lambda i: (i, 0))
    return pl.pallas_call(
        _mixer_kernel,
        grid=(n // tm,),
        in_specs=[row(ATTN_W), row(GMLP_W), row(GMLP_W), row(d), row(d), row(d),
                  _resident(ws.shape), _resident(bs_tile.shape), _resident(w_a.shape),
                  _resident(w_b.shape), _resident(w_out.shape), _resident((1, d))],
        out_specs=row(d),
        out_shape=jax.ShapeDtypeStruct((n, d), F32),
        scratch_shapes=[pltpu.VMEM((tm, GMLP_W), BF16)],
        compiler_params=_params(1),
        name="mixer",
    )(a, u, gvn, ga, gb, x, ws, bs_tile, w_a, w_b, w_out, g_post)


def _ffn_kernel(x_ref, gpre_ref, w1_ref, w2_ref, gpost_ref, o_ref):
    d_ff = w2_ref.shape[0]
    x = x_ref[...]
    h = _rms(x, gpre_ref[...]).astype(BF16)
    acc = jnp.zeros(x.shape, F32)
    for c0 in range(0, d_ff, FF_TILE):
        gate = jnp.dot(h, w1_ref[:, c0:c0 + FF_TILE], preferred_element_type=F32)
        up = jnp.dot(h, w1_ref[:, d_ff + c0:d_ff + c0 + FF_TILE], preferred_element_type=F32)
        act = (gate * jax.nn.sigmoid(gate) * up).astype(BF16)
        acc = acc + jnp.dot(act, w2_ref[c0:c0 + FF_TILE, :], preferred_element_type=F32)
    o_ref[...] = x + _rms(acc, gpost_ref[...])


def _ffn(x, g_pre, w1, w2, g_post, tm):
    n, d = x.shape
    assert w2.shape[0] % FF_TILE == 0
    row = pl.BlockSpec((tm, d), lambda i: (i, 0))
    return pl.pallas_call(
        _ffn_kernel,
        grid=(n // tm,),
        in_specs=[row, _resident((1, d)), _resident(w1.shape), _resident(w2.shape), _resident((1, d))],
        out_specs=row,
        out_shape=jax.ShapeDtypeStruct((n, d), F32),
        compiler_params=_params(1),
        name="ffn",
    )(x, g_pre, w1, w2, g_post)


def kernel(x_prompt, x_sample, cache_k, cache_v, page_table, rel_bias, g_mix_pre, w_in, lam_q1, lam_k1, lam_q2, lam_k2, g_subln, g_gmlp_v, w_spatial, b_spatial, w_branch_a, w_branch_b, w_out, g_mix_post, g_ffn_pre, w_ffn_in, w_ffn_out, g_ffn_post):
    batch, seq, d = x_prompt.shape
    nseq, t_new, _ = x_sample.shape
    depth = w_in.shape[0]
    n_s = nseq * t_new
    assert n_s % CHUNK == 0 and CHUNK % t_new == 0
    attn_tile = min(ATTN_TILE, seq)
    tm_p = attn_tile
    tm_s = CHUNK

    bias_tiles = _bias_tiles(rel_bias, attn_tile)
    xp = x_prompt.reshape(batch * seq, d)
    xs = x_sample.reshape(n_s, d)
    row = lambda v: v.reshape(1, -1)
    eye = jnp.eye(CHUNK // t_new, dtype=F32)

    outs = {name: [] for name in ("kp", "vp", "ks", "vs", "gvs")}
    for l in range(depth):
        lam_init = 0.8 - 0.6 * math.exp(-0.3 * l)
        lam_params = jnp.stack([lam_q1[l], lam_k1[l], lam_q2[l], lam_k2[l]])
        w_in_l = w_in[l].astype(BF16)
        w_a, w_b, w_o = (w_branch_a[l].astype(BF16), w_branch_b[l].astype(BF16), w_out[l].astype(BF16))
        w1, w2 = w_ffn_in[l].astype(BF16), w_ffn_out[l].astype(BF16)
        gsub = row(g_subln[l])
        ws_p = w_spatial[l].astype(BF16)
        bs_p = jnp.repeat(b_spatial[l].T, CHUNK, axis=1)
        ws_s = jax.vmap(lambda w: jnp.kron(eye, w[:t_new, :t_new]))(w_spatial[l]).astype(BF16)
        bs_s = jnp.repeat(jnp.tile(b_spatial[l][:, :t_new], (1, CHUNK // t_new)).T, CHUNK, axis=1)

        qk, vt, kp, vp, u, gvn, ga, gb = _proj(xp, row(g_mix_pre[l]), w_in_l, row(g_gmlp_v[l]), tm_p)
        a = _attn_prompt(qk, vt, bias_tiles, rel_bias, lam_params, gsub, batch, seq, lam_init)
        xp = _mixer(a, u, gvn, ga, gb, xp, ws_p, bs_p, w_a, w_b, w_o, row(g_mix_post[l]), tm_p)
        xp = _ffn(xp, row(g_ffn_pre[l]), w1, w2, row(g_ffn_post[l]), tm_p)
        outs["kp"].append(kp.reshape(batch, seq, N_HEADS, HEAD_DIM))
        outs["vp"].append(vp.reshape(batch, seq, N_HEADS, HEAD_DIM))

        qk, _, ks, vs, u, gvn, ga, gb = _proj(xs, row(g_mix_pre[l]), w_in_l, row(g_gmlp_v[l]), tm_s)
        pad_tokens = lambda t: jnp.pad(t, ((0, 0), (0, SUBLANES - t_new)) + ((0, 0),) * (t.ndim - 2))
        q8 = pad_tokens(qk[:, :ATTN_W].astype(F32).reshape(nseq, t_new, ATTN_W))
        kn = pad_tokens(ks.reshape(nseq, t_new, ATTN_W)).reshape(nseq, SUBLANES * N_HEADS, HEAD_DIM)
        vn = pad_tokens(vs.reshape(nseq, t_new, ATTN_W)).reshape(nseq, SUBLANES * N_HEADS, HEAD_DIM)
        a8 = _attn_sample(q8, kn, vn, cache_k, cache_v, page_table, l, rel_bias, lam_params, gsub,
                          t_new, lam_init)
        a = a8[:, :t_new].reshape(n_s, ATTN_W).astype(BF16)
        xs = _mixer(a, u, gvn, ga, gb, xs, ws_s, bs_s, w_a, w_b, w_o, row(g_mix_post[l]), tm_s)
        xs = _ffn(xs, row(g_ffn_pre[l]), w1, w2, row(g_ffn_post[l]), tm_s)
        outs["ks"].append(ks.reshape(nseq, t_new, N_HEADS, HEAD_DIM))
        outs["vs"].append(vs.reshape(nseq, t_new, N_HEADS, HEAD_DIM))
        outs["gvs"].append(gvn.reshape(nseq, t_new, GMLP_W))

    return (xp.reshape(batch, seq, d), xs.reshape(nseq, t_new, d),
            jnp.stack(outs["kp"]), jnp.stack(outs["vp"]),
            jnp.stack(outs["ks"]), jnp.stack(outs["vs"]), jnp.stack(outs["gvs"]))
```

```python
import functools
import math

import jax
import jax.numpy as jnp
from jax import lax
from jax.experimental import pallas as pl
from jax.experimental.pallas import tpu as pltpu

F32 = jnp.float32
BF16 = jnp.bfloat16

LANES = 128
SUBLANES = 8
BF16_ROWS = 16
VMEM_LIMIT_BYTES = 56 * 1024 * 1024

N_HEADS = 4
D_HALF = 64
HEAD_DIM = 2 * D_HALF
ATTN_W = N_HEADS * HEAD_DIM
N_GROUPS = 4
CHUNK = 128
GMLP_W = N_GROUPS * CHUNK
SCALE = D_HALF ** -0.5
MAX_EXACT = 16
MAX_DISTANCE = 128
EPS = 1e-6
NEG_INF = -1e30
LOG2E = math.log2(math.e)
VT_ROWS = HEAD_DIM + BF16_ROWS

ATTN_TILE = 512
PAGES_PER_STEP = 16
FF_TILE = 256

_NT = (((1,), (1,)), ((), ()))


def _rms(x, g):
    return x * lax.rsqrt(jnp.mean(x * x, axis=-1, keepdims=True) + EPS) * g


def _t5_bucket(dist, num_buckets):
    n = jnp.maximum(dist, 0)
    nf = jnp.maximum(n, 1).astype(F32)
    large = MAX_EXACT + (jnp.log(nf / MAX_EXACT) / math.log(MAX_DISTANCE / MAX_EXACT)
                         * (num_buckets - MAX_EXACT)).astype(jnp.int32)
    return jnp.where(n < MAX_EXACT, n, jnp.minimum(large, num_buckets - 1))


def _resident(shape):
    return pl.BlockSpec(shape, lambda *_: (0,) * len(shape), pipeline_mode=pl.Buffered(1))


def _params(n_axes):
    return pltpu.CompilerParams(dimension_semantics=("arbitrary",) * n_axes,
                                vmem_limit_bytes=VMEM_LIMIT_BYTES)


def _proj_kernel(x_ref, g_ref, w_ref, ggv_ref, *rest, n_prev):
    if n_prev:
        kprev_ref, vprev_ref = rest[:2]
        rest = rest[2:]
    qk_ref, vt_ref, k_ref, v_ref, u_ref, gvn_ref, ga_ref, gb_ref = rest
    tm, d_model = x_ref.shape
    if n_prev:
        k_ref[0:n_prev] = kprev_ref[...]
        v_ref[0:n_prev] = vprev_ref[...]
    h = _rms(x_ref[...], g_ref[...]).astype(BF16)

    def seg(start, width):
        return jnp.dot(h, w_ref[:, start:start + width], preferred_element_type=F32)

    q = seg(0, ATTN_W)
    k = seg(ATTN_W, ATTN_W)
    v = seg(2 * ATTN_W, ATTN_W)
    qk_ref[:, 0:ATTN_W] = (q * (SCALE * LOG2E)).astype(BF16)
    qk_ref[:, ATTN_W:2 * ATTN_W] = k.astype(BF16)
    vt = v.T.astype(BF16)
    pad_row = lax.broadcasted_iota(jnp.int32, (VT_ROWS - HEAD_DIM, tm), 0)
    ones_rows = jnp.where(pad_row == 0, 1.0, 0.0).astype(BF16)
    for hh in range(N_HEADS):
        vt_ref[hh * VT_ROWS:hh * VT_ROWS + HEAD_DIM, :] = vt[hh * HEAD_DIM:(hh + 1) * HEAD_DIM, :]
        vt_ref[hh * VT_ROWS + HEAD_DIM:(hh + 1) * VT_ROWS, :] = ones_rows
        k_ref[n_prev, pl.ds(hh, tm, stride=N_HEADS), :] = k[:, hh * HEAD_DIM:(hh + 1) * HEAD_DIM]
        v_ref[n_prev, pl.ds(hh, tm, stride=N_HEADS), :] = v[:, hh * HEAD_DIM:(hh + 1) * HEAD_DIM]
    u_ref[...] = seg(3 * ATTN_W, GMLP_W)
    gvn_ref[...] = _rms(seg(3 * ATTN_W + GMLP_W, GMLP_W), ggv_ref[...])
    ga_ref[...] = seg(3 * ATTN_W + 2 * GMLP_W, d_model)
    gb_ref[...] = seg(3 * ATTN_W + 2 * GMLP_W + d_model, d_model)


def _proj(x, g_pre, w_in, g_gv, tm, kv_prev=None):
    n, d = x.shape
    in_w = w_in.shape[1]
    n_prev = 0 if kv_prev is None else kv_prev[0].shape[0]
    row = lambda w: pl.BlockSpec((tm, w), lambda i: (i, 0))
    head_rows = lambda layers: pl.BlockSpec((layers, tm * N_HEADS, HEAD_DIM), lambda i: (0, i, 0))
    widths = (GMLP_W, GMLP_W, d, d)
    in_specs = [row(d), _resident((1, d)), _resident((d, in_w)), _resident((1, GMLP_W))]
    operands = (x, g_pre, w_in, g_gv)
    if n_prev:
        in_specs += [head_rows(n_prev)] * 2
        operands += tuple(kv_prev)
    kv_shape = jax.ShapeDtypeStruct((n_prev + 1, n * N_HEADS, HEAD_DIM), F32)
    return pl.pallas_call(
        functools.partial(_proj_kernel, n_prev=n_prev),
        grid=(n // tm,),
        in_specs=in_specs,
        out_specs=[row(2 * ATTN_W), pl.BlockSpec((None, N_HEADS * VT_ROWS, tm), lambda i: (i, 0, 0)),
                   head_rows(n_prev + 1), head_rows(n_prev + 1)] + [row(w) for w in widths],
        out_shape=[jax.ShapeDtypeStruct((n, 2 * ATTN_W), BF16),
                   jax.ShapeDtypeStruct((n // tm, N_HEADS * VT_ROWS, tm), BF16), kv_shape, kv_shape]
                  + [jax.ShapeDtypeStruct((n, w), F32) for w in widths],
        compiler_params=_params(1),
        name="proj",
    )(*operands)


def _bias_tile_kernel(rb_ref, o_ref, *, tile, num_buckets):
    h = pl.program_id(0)
    off = pl.program_id(1)
    key = lax.broadcasted_iota(jnp.int32, (tile, tile), 0)
    qry = lax.broadcasted_iota(jnp.int32, (tile, tile), 1)
    dist = off * tile + qry - key
    bucket = _t5_bucket(dist, num_buckets)
    bias = jnp.zeros((tile, tile), F32)
    for j in range(num_buckets):
        bias = jnp.where(bucket == j, rb_ref[j, h], bias)
    o_ref[...] = jnp.where(dist >= 0, bias * LOG2E, NEG_INF)


def _bias_tiles(rel_bias, tile):
    nb, nh = rel_bias.shape
    return pl.pallas_call(
        functools.partial(_bias_tile_kernel, tile=tile, num_buckets=nb),
        grid=(nh, 3),
        in_specs=[pl.BlockSpec(memory_space=pltpu.SMEM)],
        out_specs=pl.BlockSpec((None, None, tile, tile), lambda h, o: (h, o, 0, 0)),
        out_shape=jax.ShapeDtypeStruct((nh, 3, tile, tile), F32),
        compiler_params=_params(2),
        name="bias_tiles",
    )(rel_bias)


def _lam(lam_ref, lam_init):
    lp = lam_ref[...]
    s1 = jnp.sum(lp[0:1] * lp[1:2], axis=-1, keepdims=True)
    s2 = jnp.sum(lp[2:3] * lp[3:4], axis=-1, keepdims=True)
    return jnp.exp(s1) - jnp.exp(s2) + lam_init


def _attn_kernel(q_ref, k_ref, vt_ref, bias_ref, rb_ref, lam_ref, gsub_ref, o_ref,
                 qz_ref, st_ref, cmax_ref, shift_ref, p_ref, alpha_ref, m_ref, acc_ref,
                 *, tile, lam_init, far_bucket):
    i = pl.program_id(2)

    q = q_ref[...]
    lane = lax.broadcasted_iota(jnp.int32, q.shape, 1)
    qz_ref[0] = jnp.where(lane < D_HALF, q, jnp.zeros_like(q))
    qz_ref[1] = jnp.where(lane >= D_HALF, q, jnp.zeros_like(q))
    m_ref[...] = jnp.full(m_ref.shape, NEG_INF, F32)
    acc_ref[...] = jnp.zeros(acc_ref.shape, F32)

    far_bias = rb_ref[far_bucket, pl.program_id(1)] * LOG2E

    def scores(j, mp, far):
        kb = k_ref[pl.ds(pl.multiple_of(j * tile, tile), tile), :]
        st = lax.dot_general(kb, qz_ref[mp], _NT, preferred_element_type=F32)
        if far:
            shift = jnp.full((1, tile), far_bias, F32)
        else:
            st = st + bias_ref[jnp.minimum(i - j, 2)]
            shift = jnp.zeros((1, tile), F32)
        st_ref[mp] = st
        cmax_ref[mp] = jnp.max(st, axis=0, keepdims=True) + shift
        shift_ref[mp] = shift

    def softmax(mp):
        m_prev = m_ref[mp]
        m_new = jnp.maximum(m_prev, cmax_ref[mp])
        alpha_ref[mp] = jnp.exp2(m_prev - m_new)
        p_ref[mp] = jnp.exp2(st_ref[mp] - (m_new - shift_ref[mp])).astype(BF16)
        m_ref[mp] = m_new

    def weigh(j, mp):
        acc_ref[mp] = alpha_ref[mp] * acc_ref[mp] + jnp.dot(vt_ref[j], p_ref[mp],
                                                            preferred_element_type=F32)

    scores(0, 0, False)
    scores(0, 1, False)
    softmax(0)

    def body(j, far):
        weigh(j, 0)
        scores(j + 1, 0, far)
        softmax(1)
        weigh(j, 1)
        scores(j + 1, 1, far)
        softmax(0)

    n_far = jnp.maximum(i - 2, 0)
    lax.fori_loop(0, n_far, lambda j, c: (body(j, True), c)[1], 0)
    lax.fori_loop(n_far, i, lambda j, c: (body(j, False), c)[1], 0)
    softmax(1)
    weigh(i, 0)
    weigh(i, 1)

    lam = _lam(lam_ref, lam_init)
    a1, a2 = acc_ref[0], acc_ref[1]
    ot = (a1[:HEAD_DIM] / a1[HEAD_DIM:HEAD_DIM + 1]
          - lam * (a2[:HEAD_DIM] / a2[HEAD_DIM:HEAD_DIM + 1]))
    at = ot * lax.rsqrt(jnp.mean(ot * ot, axis=0, keepdims=True) + EPS) * gsub_ref[...]
    o_ref[...] = (at * (1.0 - lam_init)).T.astype(o_ref.dtype)


def _attn_prompt(qk, vt, bias_tiles, rel_bias, lam_params, g_subln, batch, seq, lam_init):
    tile = bias_tiles.shape[-1]
    assert tile >= MAX_DISTANCE and seq % tile == 0 and vt.shape[-1] == tile
    nq = seq // tile
    kernel = functools.partial(_attn_kernel, tile=tile, lam_init=lam_init,
                               far_bucket=rel_bias.shape[0] - 1)
    return pl.pallas_call(
        kernel,
        grid=(batch, N_HEADS, nq),
        in_specs=[
            pl.BlockSpec((tile, HEAD_DIM), lambda b, h, i: (b * nq + i, h)),
            pl.BlockSpec((seq, HEAD_DIM), lambda b, h, i: (b, N_HEADS + h)),
            pl.BlockSpec((nq, VT_ROWS, tile), lambda b, h, i: (b, h, 0)),
            pl.BlockSpec((None, 3, tile, tile), lambda b, h, i: (h, 0, 0, 0)),
            pl.BlockSpec(memory_space=pltpu.SMEM),
            pl.BlockSpec((4, D_HALF), lambda b, h, i: (0, 0)),
            pl.BlockSpec((HEAD_DIM, 1), lambda b, h, i: (0, 0)),
        ],
        out_specs=pl.BlockSpec((tile, HEAD_DIM), lambda b, h, i: (b * nq + i, h)),
        out_shape=jax.ShapeDtypeStruct((batch * seq, ATTN_W), BF16),
        scratch_shapes=[
            pltpu.VMEM((2, tile, HEAD_DIM), BF16),
            pltpu.VMEM((2, tile, tile), F32),
            pltpu.VMEM((2, 1, tile), F32),
            pltpu.VMEM((2, 1, tile), F32),
            pltpu.VMEM((2, tile, tile), BF16),
            pltpu.VMEM((2, 1, tile), F32),
            pltpu.VMEM((2, 1, tile), F32),
            pltpu.VMEM((2, VT_ROWS, tile), F32),
        ],
        compiler_params=_params(3),
        name="attn_prompt",
    )(qk, qk, vt, bias_tiles, rel_bias, lam_params, g_subln.reshape(HEAD_DIM, 1))


def _sattn_kernel(pt_ref, q_ref, kn_ref, vn_ref, lam_ref, gsub_ref, rb_ref, *rest,
                  gp, page, t_new, lam_init, num_buckets):
    del pt_ref
    k_refs, v_refs = rest[:gp], rest[gp:2 * gp]
    o_ref, qs_ref, s_ref, bias_ref, m_ref, l_ref, acc_ref = rest[2 * gp:]
    g = pl.program_id(1)
    n_groups = pl.num_programs(1) - 1
    head_rows = 2 * SUBLANES
    rows = N_HEADS * head_rows
    pcols = page * N_HEADS
    cols = gp * pcols

    r = lax.broadcasted_iota(jnp.int32, (rows, 1), 0)
    hrow = r // head_rows
    trow = r % SUBLANES

    def bias_column(bucket):
        col = jnp.full((rows, 1), rb_ref[bucket, 0], F32)
        for hh in range(1, N_HEADS):
            col = jnp.where(hrow == hh, rb_ref[bucket, hh], col)
        return col * LOG2E

    @pl.when(g == 0)
    def _():
        q8 = q_ref[...]
        lane = lax.broadcasted_iota(jnp.int32, (SUBLANES, HEAD_DIM), 1)
        for hh in range(N_HEADS):
            qh = q8[:, hh * HEAD_DIM:(hh + 1) * HEAD_DIM]
            qs_ref[hh * head_rows:hh * head_rows + SUBLANES, :] = jnp.where(lane < D_HALF, qh, 0.0)
            qs_ref[hh * head_rows + SUBLANES:(hh + 1) * head_rows, :] = jnp.where(lane >= D_HALF, qh, 0.0)
        m_ref[...] = jnp.full(m_ref.shape, NEG_INF, F32)
        l_ref[...] = jnp.zeros(l_ref.shape, F32)
        acc_ref[...] = jnp.zeros(acc_ref.shape, F32)
        chead = lax.broadcasted_iota(jnp.int32, (rows, cols), 1) % N_HEADS
        bias_ref[...] = jnp.where(chead == hrow, bias_column(num_buckets - 1), NEG_INF)
        s_ref[1] = jnp.full(s_ref.shape[1:], -jnp.inf, F32)

    @pl.when(g == n_groups)
    def _():
        c = lax.broadcasted_iota(jnp.int32, (rows, pcols), 1)
        dist = trow + (page - c // N_HEADS)
        bucket = _t5_bucket(dist, num_buckets)
        b = jnp.zeros(dist.shape, F32)
        for j in range(num_buckets):
            b = jnp.where(bucket == j, bias_column(j), b)
        ok = (c % N_HEADS == hrow) & (dist >= 0)
        bias_ref[:, cols - pcols:] = jnp.where(ok, b, NEG_INF)

    def stages(cur, prev):
        qs = qs_ref[...]
        for j in range(gp):
            s_ref[cur, :, j * pcols:(j + 1) * pcols] = lax.dot_general(
                qs, k_refs[j][...], _NT, preferred_element_type=F32)

        s = [s_ref[prev, :, j * pcols:(j + 1) * pcols] + bias_ref[:, j * pcols:(j + 1) * pcols]
             for j in range(gp)]
        m_prev = m_ref[...]
        m_new = m_prev
        for sj in s:
            m_new = jnp.maximum(m_new, jnp.max(sj, axis=-1, keepdims=True))
        alpha = jnp.exp2(m_prev - m_new)
        l_new = alpha * l_ref[...]
        acc = alpha * acc_ref[...]
        for j in range(gp):
            p = jnp.exp2(s[j] - m_new)
            l_new = l_new + jnp.sum(p, axis=-1, keepdims=True)
            acc = acc + jnp.dot(p, v_refs[j][...], preferred_element_type=F32)
        l_ref[...] = l_new
        acc_ref[...] = acc
        m_ref[...] = m_new

    @pl.when(g % 2 == 0)
    def _():
        stages(0, 1)

    @pl.when(g % 2 == 1)
    def _():
        stages(1, 0)

    @pl.when(g == n_groups)
    def _():
        lane = lax.broadcasted_iota(jnp.int32, (head_rows, LANES), 1)
        dist = lax.broadcasted_iota(jnp.int32, (head_rows, 1), 0) % SUBLANES - lane
        valid = (dist >= 0) & (lane < t_new)
        bucket = _t5_bucket(dist, num_buckets)
        lam = _lam(lam_ref, lam_init)
        for hh in range(N_HEADS):
            rws = slice(hh * head_rows, (hh + 1) * head_rows)
            b = jnp.zeros((head_rows, LANES), F32)
            for j in range(min(num_buckets, MAX_EXACT)):
                b = jnp.where(bucket == j, rb_ref[j, hh] * LOG2E, b)
            qh = qs_ref[rws, :]
            s_new = jnp.zeros((head_rows, LANES), F32)
            for t in range(t_new):
                krow = kn_ref[t * N_HEADS + hh:t * N_HEADS + hh + 1, :]
                s_new = jnp.where(lane == t, jnp.sum(qh * krow, axis=-1, keepdims=True), s_new)
            s_new = jnp.where(valid, s_new + b, NEG_INF)
            m_prev = m_ref[rws, :]
            m_fin = jnp.maximum(m_prev, jnp.max(s_new, axis=-1, keepdims=True))
            alpha = jnp.exp2(m_prev - m_fin)
            p_new = jnp.exp2(s_new - m_fin)
            l_fin = alpha * l_ref[rws, :] + jnp.sum(p_new, axis=-1, keepdims=True)
            acc = alpha * acc_ref[rws, :]
            for t in range(t_new):
                pt = jnp.sum(jnp.where(lane == t, p_new, 0.0), axis=-1, keepdims=True)
                acc = acc + pt * vn_ref[t * N_HEADS + hh:t * N_HEADS + hh + 1, :]
            o = acc / l_fin
            diff = o[:SUBLANES] - lam * o[SUBLANES:]
            o_ref[:, hh * HEAD_DIM:(hh + 1) * HEAD_DIM] = _rms(diff, gsub_ref[...]) * (1.0 - lam_init)


def _attn_sample(q8, kn, vn, cache_k, cache_v, page_table, layer, rel_bias, lam_params, g_subln,
                 t_new, lam_init):
    nseq = q8.shape[0]
    depth, n_pool, page = cache_k.shape[:3]
    n_pages = page_table.shape[1]
    gp = min(PAGES_PER_STEP, n_pages)
    assert n_pages % gp == 0 and t_new <= SUBLANES
    assert page >= MAX_DISTANCE and rel_bias.shape[0] >= MAX_EXACT
    ck = cache_k.reshape(depth * n_pool, page * N_HEADS, HEAD_DIM)
    cv = cache_v.reshape(depth * n_pool, page * N_HEADS, HEAD_DIM)
    base = layer * n_pool
    rows = N_HEADS * 2 * SUBLANES
    cols = gp * page * N_HEADS
    nb = rel_bias.shape[0]

    n_groups = n_pages // gp

    def page_spec(j, lag):
        def index(b, g, pt):
            grp = jnp.clip(g - lag, 0, n_groups - 1)
            return (base + pt[b * n_pages + grp * gp + j], 0, 0)
        return pl.BlockSpec((None, page * N_HEADS, HEAD_DIM), index)

    seq_spec = pl.BlockSpec((None, SUBLANES, ATTN_W), lambda b, g, pt: (b, 0, 0))
    new_spec = pl.BlockSpec((None, SUBLANES * N_HEADS, HEAD_DIM), lambda b, g, pt: (b, 0, 0))
    kernel = functools.partial(_sattn_kernel, gp=gp, page=page, t_new=t_new, lam_init=lam_init,
                               num_buckets=nb)
    grid_spec = pltpu.PrefetchScalarGridSpec(
        num_scalar_prefetch=1,
        grid=(nseq, n_groups + 1),
        in_specs=[seq_spec, new_spec, new_spec,
                  pl.BlockSpec((4, D_HALF), lambda b, g, pt: (0, 0)),
                  pl.BlockSpec((1, HEAD_DIM), lambda b, g, pt: (0, 0)),
                  pl.BlockSpec(memory_space=pltpu.SMEM)]
                 + [page_spec(j, 0) for j in range(gp)] + [page_spec(j, 1) for j in range(gp)],
        out_specs=seq_spec,
        scratch_shapes=[
            pltpu.VMEM((rows, HEAD_DIM), F32),
            pltpu.VMEM((2, rows, cols), F32),
            pltpu.VMEM((rows, cols), F32),
            pltpu.VMEM((rows, 1), F32),
            pltpu.VMEM((rows, 1), F32),
            pltpu.VMEM((rows, HEAD_DIM), F32),
        ],
    )
    return pl.pallas_call(
        kernel,
        grid_spec=grid_spec,
        out_shape=jax.ShapeDtypeStruct((nseq, SUBLANES, ATTN_W), F32),
        compiler_params=_params(2),
        name="attn_sample",
    )(page_table.reshape(-1), q8, kn, vn, lam_params, g_subln, rel_bias,
      *([ck] * gp), *([cv] * gp))


def _mixer_kernel(a_ref, u_ref, gvn_ref, ga_ref, gb_ref, x_ref, ws_ref, bs_ref, wa_ref, wb_ref,
                  wo_ref, gpost_ref, o_ref, b_scr):
    tm = x_ref.shape[0]
    r = lax.broadcasted_iota(jnp.int32, (CHUNK, CHUNK), 0)
    c = lax.broadcasted_iota(jnp.int32, (CHUNK, CHUNK), 1)
    causal = r >= c
    for grp in range(N_GROUPS):
        cols = slice(grp * CHUNK, (grp + 1) * CHUNK)
        wg = ws_ref[grp]
        wg = jnp.where(causal, wg, jnp.zeros_like(wg))
        for ch in range(tm // CHUNK):
            rws = slice(ch * CHUNK, (ch + 1) * CHUNK)
            sp = jnp.dot(wg, gvn_ref[rws, cols].astype(BF16), preferred_element_type=F32)
            sp = sp + bs_ref[:, cols]
            b_scr[rws, cols] = (u_ref[rws, cols] * sp).astype(BF16)
    ya = jnp.dot(a_ref[...], wa_ref[...], preferred_element_type=F32)
    yb = jnp.dot(b_scr[...], wb_ref[...], preferred_element_type=F32)
    mix = jax.nn.sigmoid(ga_ref[...]) * ya + jax.nn.sigmoid(gb_ref[...]) * yb
    mo = jnp.dot(mix.astype(BF16), wo_ref[...], preferred_element_type=F32)
    o_ref[...] = x_ref[...] + _rms(mo, gpost_ref[...])


def _mixer(a, u, gvn, ga, gb, x, ws, bs_tile, w_a, w_b, w_out, g_post, tm):
    n, d = x.shape
    row = lambda w: pl.BlockSpec((tm, w), lambda i: (i, 0))
    return pl.pallas_call(
        _mixer_kernel,
        grid=(n // tm,),
        in_specs=[row(ATTN_W), row(GMLP_W), row(GMLP_W), row(d), row(d), row(d),
                  _resident(ws.shape), _resident(bs_tile.shape), _resident(w_a.shape),
                  _resident(w_b.shape), _resident(w_out.shape), _resident((1, d))],
        out_specs=row(d),
        out_shape=jax.ShapeDtypeStruct((n, d), F32),
        scratch_shapes=[pltpu.VMEM((tm, GMLP_W), BF16)],
        compiler_params=_params(1),
        name="mixer",
    )(a, u, gvn, ga, gb, x, ws, bs_tile, w_a, w_b, w_out, g_post)


def _ffn_kernel(x_ref, gpre_ref, w1_ref, w2_ref, gpost_ref, o_ref):
    d_ff = w2_ref.shape[0]
    x = x_ref[...]
    h = _rms(x, gpre_ref[...]).astype(BF16)
    acc = jnp.zeros(x.shape, F32)
    for c0 in range(0, d_ff, FF_TILE):
        gate = jnp.dot(h, w1_ref[:, c0:c0 + FF_TILE], preferred_element_type=F32)
        up = jnp.dot(h, w1_ref[:, d_ff + c0:d_ff + c0 + FF_TILE], preferred_element_type=F32)
        act = (gate * jax.nn.sigmoid(gate) * up).astype(BF16)
        acc = acc + jnp.dot(act, w2_ref[c0:c0 + FF_TILE, :], preferred_element_type=F32)
    o_ref[...] = x + _rms(acc, gpost_ref[...])


def _ffn(x, g_pre, w1, w2, g_post, tm):
    n, d = x.shape
    assert w2.shape[0] % FF_TILE == 0
    row = pl.BlockSpec((tm, d), lambda i: (i, 0))
    return pl.pallas_call(
        _ffn_kernel,
        grid=(n // tm,),
        in_specs=[row, _resident((1, d)), _resident(w1.shape), _resident(w2.shape), _resident((1, d))],
        out_specs=row,
        out_shape=jax.ShapeDtypeStruct((n, d), F32),
        compiler_params=_params(1),
        name="ffn",
    )(x, g_pre, w1, w2, g_post)


def kernel(x_prompt, x_sample, cache_k, cache_v, page_table, rel_bias, g_mix_pre, w_in, lam_q1, lam_k1, lam_q2, lam_k2, g_subln, g_gmlp_v, w_spatial, b_spatial, w_branch_a, w_branch_b, w_out, g_mix_post, g_ffn_pre, w_ffn_in, w_ffn_out, g_ffn_post):
    batch, seq, d = x_prompt.shape
    nseq, t_new, _ = x_sample.shape
    depth = w_in.shape[0]
    n_s = nseq * t_new
    assert n_s % CHUNK == 0 and CHUNK % t_new == 0
    attn_tile = min(ATTN_TILE, seq)
    tm_p = attn_tile
    tm_s = CHUNK

    bias_tiles = _bias_tiles(rel_bias, attn_tile)
    xp = x_prompt.reshape(batch * seq, d)
    xs = x_sample.reshape(n_s, d)
    row = lambda v: v.reshape(1, -1)
    eye = jnp.eye(CHUNK // t_new, dtype=F32)

    outs = {name: [] for name in ("gvs",)}
    kv_p = kv_s = None
    for l in range(depth):
        lam_init = 0.8 - 0.6 * math.exp(-0.3 * l)
        lam_params = jnp.stack([lam_q1[l], lam_k1[l], lam_q2[l], lam_k2[l]])
        w_in_l = w_in[l].astype(BF16)
        w_a, w_b, w_o = (w_branch_a[l].astype(BF16), w_branch_b[l].astype(BF16), w_out[l].astype(BF16))
        w1, w2 = w_ffn_in[l].astype(BF16), w_ffn_out[l].astype(BF16)
        gsub = row(g_subln[l])
        ws_p = w_spatial[l].astype(BF16)
        bs_p = jnp.repeat(b_spatial[l].T, CHUNK, axis=1)
        ws_s = jax.vmap(lambda w: jnp.kron(eye, w[:t_new, :t_new]))(w_spatial[l]).astype(BF16)
        bs_s = jnp.repeat(jnp.tile(b_spatial[l][:, :t_new], (1, CHUNK // t_new)).T, CHUNK, axis=1)

        qk, vt, kp, vp, u, gvn, ga, gb = _proj(xp, row(g_mix_pre[l]), w_in_l, row(g_gmlp_v[l]), tm_p, kv_p)
        kv_p = (kp, vp)
        a = _attn_prompt(qk, vt, bias_tiles, rel_bias, lam_params, gsub, batch, seq, lam_init)
        xp = _mixer(a, u, gvn, ga, gb, xp, ws_p, bs_p, w_a, w_b, w_o, row(g_mix_post[l]), tm_p)
        xp = _ffn(xp, row(g_ffn_pre[l]), w1, w2, row(g_ffn_post[l]), tm_p)

        qk, _, ks, vs, u, gvn, ga, gb = _proj(xs, row(g_mix_pre[l]), w_in_l, row(g_gmlp_v[l]), tm_s, kv_s)
        kv_s = (ks, vs)
        pad_tokens = lambda t: jnp.pad(t, ((0, 0), (0, SUBLANES - t_new)) + ((0, 0),) * (t.ndim - 2))
        q8 = pad_tokens(qk[:, :ATTN_W].astype(F32).reshape(nseq, t_new, ATTN_W))
        kn = pad_tokens(ks[l].reshape(nseq, t_new, ATTN_W)).reshape(nseq, SUBLANES * N_HEADS, HEAD_DIM)
        vn = pad_tokens(vs[l].reshape(nseq, t_new, ATTN_W)).reshape(nseq, SUBLANES * N_HEADS, HEAD_DIM)
        a8 = _attn_sample(q8, kn, vn, cache_k, cache_v, page_table, l, rel_bias, lam_params, gsub,
                          t_new, lam_init)
        a = a8[:, :t_new].reshape(n_s, ATTN_W).astype(BF16)
        xs = _mixer(a, u, gvn, ga, gb, xs, ws_s, bs_s, w_a, w_b, w_o, row(g_mix_post[l]), tm_s)
        xs = _ffn(xs, row(g_ffn_pre[l]), w1, w2, row(g_ffn_post[l]), tm_s)
        outs["gvs"].append(gvn.reshape(nseq, t_new, GMLP_W))

    return (xp.reshape(batch, seq, d), xs.reshape(nseq, t_new, d),
            kv_p[0].reshape(depth, batch, seq, N_HEADS, HEAD_DIM),
            kv_p[1].reshape(depth, batch, seq, N_HEADS, HEAD_DIM),
            kv_s[0].reshape(depth, nseq, t_new, N_HEADS, HEAD_DIM),
            kv_s[1].reshape(depth, nseq, t_new, N_HEADS, HEAD_DIM), jnp.stack(outs["gvs"]))
```

```python
import functools
import math

import jax
import jax.numpy as jnp
from jax import lax
from jax.experimental import pallas as pl
from jax.experimental.pallas import tpu as pltpu

F32 = jnp.float32
BF16 = jnp.bfloat16

LANES = 128
SUBLANES = 8
BF16_ROWS = 16
VMEM_LIMIT_BYTES = 56 * 1024 * 1024

N_HEADS = 4
D_HALF = 64
HEAD_DIM = 2 * D_HALF
ATTN_W = N_HEADS * HEAD_DIM
N_GROUPS = 4
CHUNK = 128
GMLP_W = N_GROUPS * CHUNK
SCALE = D_HALF ** -0.5
MAX_EXACT = 16
MAX_DISTANCE = 128
EPS = 1e-6
NEG_INF = -1e30
LOG2E = math.log2(math.e)
VT_ROWS = HEAD_DIM + BF16_ROWS

ATTN_TILE = 512
PAGES_PER_STEP = 16
FF_TILE = 256

_NT = (((1,), (1,)), ((), ()))


def _rms(x, g):
    return x * lax.rsqrt(jnp.mean(x * x, axis=-1, keepdims=True) + EPS) * g


def _t5_bucket(dist, num_buckets):
    n = jnp.maximum(dist, 0)
    nf = jnp.maximum(n, 1).astype(F32)
    large = MAX_EXACT + (jnp.log(nf / MAX_EXACT) / math.log(MAX_DISTANCE / MAX_EXACT)
                         * (num_buckets - MAX_EXACT)).astype(jnp.int32)
    return jnp.where(n < MAX_EXACT, n, jnp.minimum(large, num_buckets - 1))


def _resident(shape):
    return pl.BlockSpec(shape, lambda *_: (0,) * len(shape), pipeline_mode=pl.Buffered(1))


def _params(n_axes):
    return pltpu.CompilerParams(dimension_semantics=("arbitrary",) * n_axes,
                                vmem_limit_bytes=VMEM_LIMIT_BYTES)


def _proj_kernel(x_ref, g_ref, w_ref, ggv_ref, *rest, n_prev):
    if n_prev:
        kprev_ref, vprev_ref = rest[:2]
        rest = rest[2:]
    qk_ref, vt_ref, k_ref, v_ref, u_ref, gvn_ref, ga_ref, gb_ref = rest
    tm, d_model = x_ref.shape
    if n_prev:
        k_ref[0:n_prev] = kprev_ref[...]
        v_ref[0:n_prev] = vprev_ref[...]
    h = _rms(x_ref[...], g_ref[...]).astype(BF16)

    def seg(start, width):
        return jnp.dot(h, w_ref[:, start:start + width], preferred_element_type=F32)

    q = seg(0, ATTN_W)
    k = seg(ATTN_W, ATTN_W)
    v = seg(2 * ATTN_W, ATTN_W)
    qk_ref[:, 0:ATTN_W] = (q * (SCALE * LOG2E)).astype(BF16)
    qk_ref[:, ATTN_W:2 * ATTN_W] = k.astype(BF16)
    vt = v.T.astype(BF16)
    pad_row = lax.broadcasted_iota(jnp.int32, (VT_ROWS - HEAD_DIM, tm), 0)
    ones_rows = jnp.where(pad_row == 0, 1.0, 0.0).astype(BF16)
    for hh in range(N_HEADS):
        vt_ref[hh * VT_ROWS:hh * VT_ROWS + HEAD_DIM, :] = vt[hh * HEAD_DIM:(hh + 1) * HEAD_DIM, :]
        vt_ref[hh * VT_ROWS + HEAD_DIM:(hh + 1) * VT_ROWS, :] = ones_rows
        k_ref[n_prev, pl.ds(hh, tm, stride=N_HEADS), :] = k[:, hh * HEAD_DIM:(hh + 1) * HEAD_DIM]
        v_ref[n_prev, pl.ds(hh, tm, stride=N_HEADS), :] = v[:, hh * HEAD_DIM:(hh + 1) * HEAD_DIM]
    u_ref[...] = seg(3 * ATTN_W, GMLP_W)
    gvn_ref[...] = _rms(seg(3 * ATTN_W + GMLP_W, GMLP_W), ggv_ref[...])
    ga_ref[...] = seg(3 * ATTN_W + 2 * GMLP_W, d_model)
    gb_ref[...] = seg(3 * ATTN_W + 2 * GMLP_W + d_model, d_model)


def _proj(x, g_pre, w_in, g_gv, tm, kv_prev=None):
    n, d = x.shape
    in_w = w_in.shape[1]
    n_prev = 0 if kv_prev is None else kv_prev[0].shape[0]
    row = lambda w: pl.BlockSpec((tm, w), lambda i: (i, 0))
    head_rows = lambda layers: pl.BlockSpec((layers, tm * N_HEADS, HEAD_DIM), lambda i: (0, i, 0))
    widths = (GMLP_W, GMLP_W, d, d)
    in_specs = [row(d), _resident((1, d)), _resident((d, in_w)), _resident((1, GMLP_W))]
    operands = (x, g_pre, w_in, g_gv)
    if n_prev:
        in_specs += [head_rows(n_prev)] * 2
        operands += tuple(kv_prev)
    kv_shape = jax.ShapeDtypeStruct((n_prev + 1, n * N_HEADS, HEAD_DIM), F32)
    return pl.pallas_call(
        functools.partial(_proj_kernel, n_prev=n_prev),
        grid=(n // tm,),
        in_specs=in_specs,
        out_specs=[row(2 * ATTN_W), pl.BlockSpec((None, N_HEADS * VT_ROWS, tm), lambda i: (i, 0, 0)),
                   head_rows(n_prev + 1), head_rows(n_prev + 1)] + [row(w) for w in widths],
        out_shape=[jax.ShapeDtypeStruct((n, 2 * ATTN_W), BF16),
                   jax.ShapeDtypeStruct((n // tm, N_HEADS * VT_ROWS, tm), BF16), kv_shape, kv_shape]
                  + [jax.ShapeDtypeStruct((n, w), F32) for w in widths],
        compiler_params=_params(1),
        name="proj",
    )(*operands)


def _bias_tile_kernel(rb_ref, o_ref, *, tile, num_buckets):
    h = pl.program_id(0)
    off = pl.program_id(1)
    key = lax.broadcasted_iota(jnp.int32, (tile, tile), 0)
    qry = lax.broadcasted_iota(jnp.int32, (tile, tile), 1)
    dist = off * tile + qry - key
    bucket = _t5_bucket(dist, num_buckets)
    bias = jnp.zeros((tile, tile), F32)
    for j in range(num_buckets):
        bias = jnp.where(bucket == j, rb_ref[j, h], bias)
    o_ref[...] = jnp.where(dist >= 0, bias * LOG2E, NEG_INF)


def _bias_tiles(rel_bias, tile):
    nb, nh = rel_bias.shape
    return pl.pallas_call(
        functools.partial(_bias_tile_kernel, tile=tile, num_buckets=nb),
        grid=(nh, 3),
        in_specs=[pl.BlockSpec(memory_space=pltpu.SMEM)],
        out_specs=pl.BlockSpec((None, None, tile, tile), lambda h, o: (h, o, 0, 0)),
        out_shape=jax.ShapeDtypeStruct((nh, 3, tile, tile), F32),
        compiler_params=_params(2),
        name="bias_tiles",
    )(rel_bias)


def _lam(lam_ref, lam_init):
    lp = lam_ref[...]
    s1 = jnp.sum(lp[0:1] * lp[1:2], axis=-1, keepdims=True)
    s2 = jnp.sum(lp[2:3] * lp[3:4], axis=-1, keepdims=True)
    return jnp.exp(s1) - jnp.exp(s2) + lam_init


def _attn_kernel(q_ref, k_ref, vt_ref, bias_ref, rb_ref, lam_ref, gsub_ref, o_ref,
                 qz_ref, st_ref, cmax_ref, shift_ref, p_ref, alpha_ref, m_ref, acc_ref,
                 *, tile, lam_init, far_bucket):
    i = pl.program_id(2)

    q = q_ref[...]
    lane = lax.broadcasted_iota(jnp.int32, q.shape, 1)
    qz_ref[0] = jnp.where(lane < D_HALF, q, jnp.zeros_like(q))
    qz_ref[1] = jnp.where(lane >= D_HALF, q, jnp.zeros_like(q))
    m_ref[...] = jnp.full(m_ref.shape, NEG_INF, F32)
    acc_ref[...] = jnp.zeros(acc_ref.shape, F32)

    far_bias = rb_ref[far_bucket, pl.program_id(1)] * LOG2E

    def scores(j, mp, far):
        kb = k_ref[pl.ds(pl.multiple_of(j * tile, tile), tile), :]
        st = lax.dot_general(kb, qz_ref[mp], _NT, preferred_element_type=F32)
        if far:
            shift = jnp.full((1, tile), far_bias, F32)
        else:
            st = st + bias_ref[jnp.minimum(i - j, 2)]
            shift = jnp.zeros((1, tile), F32)
        st_ref[mp] = st
        cmax_ref[mp] = jnp.max(st, axis=0, keepdims=True) + shift
        shift_ref[mp] = shift

    def softmax(mp):
        m_prev = m_ref[mp]
        m_new = jnp.maximum(m_prev, cmax_ref[mp])
        alpha_ref[mp] = jnp.exp2(m_prev - m_new)
        p_ref[mp] = jnp.exp2(st_ref[mp] - (m_new - shift_ref[mp])).astype(BF16)
        m_ref[mp] = m_new

    def weigh(j, mp):
        acc_ref[mp] = alpha_ref[mp] * acc_ref[mp] + jnp.dot(vt_ref[j], p_ref[mp],
                                                            preferred_element_type=F32)

    scores(0, 0, False)
    scores(0, 1, False)
    softmax(0)

    def body(j, far):
        weigh(j, 0)
        scores(j + 1, 0, far)
        softmax(1)
        weigh(j, 1)
        scores(j + 1, 1, far)
        softmax(0)

    n_far = jnp.maximum(i - 2, 0)
    lax.fori_loop(0, n_far, lambda j, c: (body(j, True), c)[1], 0)
    lax.fori_loop(n_far, i, lambda j, c: (body(j, False), c)[1], 0)
    weigh(i, 0)
    softmax(1)
    weigh(i, 1)

    lam = _lam(lam_ref, lam_init)
    a1, a2 = acc_ref[0], acc_ref[1]
    ot = (a1[:HEAD_DIM] / a1[HEAD_DIM:HEAD_DIM + 1]
          - lam * (a2[:HEAD_DIM] / a2[HEAD_DIM:HEAD_DIM + 1]))
    at = ot * lax.rsqrt(jnp.mean(ot * ot, axis=0, keepdims=True) + EPS) * gsub_ref[...]
    o_ref[...] = (at * (1.0 - lam_init)).T.astype(o_ref.dtype)


def _attn_prompt(qk, vt, bias_tiles, rel_bias, lam_params, g_subln, batch, seq, lam_init):
    tile = bias_tiles.shape[-1]
    assert tile >= MAX_DISTANCE and seq % tile == 0 and vt.shape[-1] == tile
    nq = seq // tile
    kernel = functools.partial(_attn_kernel, tile=tile, lam_init=lam_init,
                               far_bucket=rel_bias.shape[0] - 1)
    return pl.pallas_call(
        kernel,
        grid=(batch, N_HEADS, nq),
        in_specs=[
            pl.BlockSpec((tile, HEAD_DIM), lambda b, h, i: (b * nq + i, h)),
            pl.BlockSpec((seq, HEAD_DIM), lambda b, h, i: (b, N_HEADS + h)),
            pl.BlockSpec((nq, VT_ROWS, tile), lambda b, h, i: (b, h, 0)),
            pl.BlockSpec((None, 3, tile, tile), lambda b, h, i: (h, 0, 0, 0)),
            pl.BlockSpec(memory_space=pltpu.SMEM),
            pl.BlockSpec((4, D_HALF), lambda b, h, i: (0, 0)),
            pl.BlockSpec((HEAD_DIM, 1), lambda b, h, i: (0, 0)),
        ],
        out_specs=pl.BlockSpec((tile, HEAD_DIM), lambda b, h, i: (b * nq + i, h)),
        out_shape=jax.ShapeDtypeStruct((batch * seq, ATTN_W), BF16),
        scratch_shapes=[
            pltpu.VMEM((2, tile, HEAD_DIM), BF16),
            pltpu.VMEM((2, tile, tile), F32),
            pltpu.VMEM((2, 1, tile), F32),
            pltpu.VMEM((2, 1, tile), F32),
            pltpu.VMEM((2, tile, tile), BF16),
            pltpu.VMEM((2, 1, tile), F32),
            pltpu.VMEM((2, 1, tile), F32),
            pltpu.VMEM((2, VT_ROWS, tile), F32),
        ],
        compiler_params=_params(3),
        name="attn_prompt",
    )(qk, qk, vt, bias_tiles, rel_bias, lam_params, g_subln.reshape(HEAD_DIM, 1))


def _sattn_kernel(pt_ref, q_ref, kn_ref, vn_ref, lam_ref, gsub_ref, rb_ref, *rest,
                  gp, page, t_new, lam_init, num_buckets):
    del pt_ref
    k_refs, v_refs = rest[:gp], rest[gp:2 * gp]
    o_ref, qs_ref, s_ref, bias_ref, m_ref, l_ref, acc_ref = rest[2 * gp:]
    g = pl.program_id(1)
    n_groups = pl.num_programs(1) - 1
    head_rows = 2 * SUBLANES
    rows = N_HEADS * head_rows
    pcols = page * N_HEADS
    cols = gp * pcols

    r = lax.broadcasted_iota(jnp.int32, (rows, 1), 0)
    hrow = r // head_rows
    trow = r % SUBLANES

    def bias_column(bucket):
        col = jnp.full((rows, 1), rb_ref[bucket, 0], F32)
        for hh in range(1, N_HEADS):
            col = jnp.where(hrow == hh, rb_ref[bucket, hh], col)
        return col * LOG2E

    @pl.when(g == 0)
    def _():
        q8 = q_ref[...]
        lane = lax.broadcasted_iota(jnp.int32, (SUBLANES, HEAD_DIM), 1)
        for hh in range(N_HEADS):
            qh = q8[:, hh * HEAD_DIM:(hh + 1) * HEAD_DIM]
            qs_ref[hh * head_rows:hh * head_rows + SUBLANES, :] = jnp.where(lane < D_HALF, qh, 0.0)
            qs_ref[hh * head_rows + SUBLANES:(hh + 1) * head_rows, :] = jnp.where(lane >= D_HALF, qh, 0.0)
        m_ref[...] = jnp.full(m_ref.shape, NEG_INF, F32)
        l_ref[...] = jnp.zeros(l_ref.shape, F32)
        acc_ref[...] = jnp.zeros(acc_ref.shape, F32)
        chead = lax.broadcasted_iota(jnp.int32, (rows, cols), 1) % N_HEADS
        bias_ref[...] = jnp.where(chead == hrow, bias_column(num_buckets - 1), NEG_INF)
        s_ref[1] = jnp.full(s_ref.shape[1:], -jnp.inf, F32)

    @pl.when(g == n_groups)
    def _():
        c = lax.broadcasted_iota(jnp.int32, (rows, pcols), 1)
        dist = trow + (page - c // N_HEADS)
        bucket = _t5_bucket(dist, num_buckets)
        b = jnp.zeros(dist.shape, F32)
        for j in range(num_buckets):
            b = jnp.where(bucket == j, bias_column(j), b)
        ok = (c % N_HEADS == hrow) & (dist >= 0)
        bias_ref[:, cols - pcols:] = jnp.where(ok, b, NEG_INF)

    def stages(cur, prev):
        qs = qs_ref[...]
        for j in range(gp):
            s_ref[cur, :, j * pcols:(j + 1) * pcols] = lax.dot_general(
                qs, k_refs[j][...], _NT, preferred_element_type=F32)

        s = [s_ref[prev, :, j * pcols:(j + 1) * pcols] + bias_ref[:, j * pcols:(j + 1) * pcols]
             for j in range(gp)]
        m_prev = m_ref[...]
        m_new = m_prev
        for sj in s:
            m_new = jnp.maximum(m_new, jnp.max(sj, axis=-1, keepdims=True))
        alpha = jnp.exp2(m_prev - m_new)
        l_new = alpha * l_ref[...]
        acc = alpha * acc_ref[...]
        for j in range(gp):
            p = jnp.exp2(s[j] - m_new)
            l_new = l_new + jnp.sum(p, axis=-1, keepdims=True)
            acc = acc + jnp.dot(p, v_refs[j][...], preferred_element_type=F32)
        l_ref[...] = l_new
        acc_ref[...] = acc
        m_ref[...] = m_new

    @pl.when(g % 2 == 0)
    def _():
        stages(0, 1)

    @pl.when(g % 2 == 1)
    def _():
        stages(1, 0)

    @pl.when(g == n_groups)
    def _():
        lane = lax.broadcasted_iota(jnp.int32, (head_rows, LANES), 1)
        dist = lax.broadcasted_iota(jnp.int32, (head_rows, 1), 0) % SUBLANES - lane
        valid = (dist >= 0) & (lane < t_new)
        bucket = _t5_bucket(dist, num_buckets)
        lam = _lam(lam_ref, lam_init)
        for hh in range(N_HEADS):
            rws = slice(hh * head_rows, (hh + 1) * head_rows)
            b = jnp.zeros((head_rows, LANES), F32)
            for j in range(min(num_buckets, MAX_EXACT)):
                b = jnp.where(bucket == j, rb_ref[j, hh] * LOG2E, b)
            qh = qs_ref[rws, :]
            s_new = jnp.zeros((head_rows, LANES), F32)
            for t in range(t_new):
                krow = kn_ref[t * N_HEADS + hh:t * N_HEADS + hh + 1, :]
                s_new = jnp.where(lane == t, jnp.sum(qh * krow, axis=-1, keepdims=True), s_new)
            s_new = jnp.where(valid, s_new + b, NEG_INF)
            m_prev = m_ref[rws, :]
            m_fin = jnp.maximum(m_prev, jnp.max(s_new, axis=-1, keepdims=True))
            alpha = jnp.exp2(m_prev - m_fin)
            p_new = jnp.exp2(s_new - m_fin)
            l_fin = alpha * l_ref[rws, :] + jnp.sum(p_new, axis=-1, keepdims=True)
            acc = alpha * acc_ref[rws, :]
            for t in range(t_new):
                pt = jnp.sum(jnp.where(lane == t, p_new, 0.0), axis=-1, keepdims=True)
                acc = acc + pt * vn_ref[t * N_HEADS + hh:t * N_HEADS + hh + 1, :]
            o = acc / l_fin
            diff = o[:SUBLANES] - lam * o[SUBLANES:]
            o_ref[:, hh * HEAD_DIM:(hh + 1) * HEAD_DIM] = _rms(diff, gsub_ref[...]) * (1.0 - lam_init)


def _attn_sample(q8, kn, vn, cache_k, cache_v, page_table, layer, rel_bias, lam_params, g_subln,
                 t_new, lam_init):
    nseq = q8.shape[0]
    depth, n_pool, page = cache_k.shape[:3]
    n_pages = page_table.shape[1]
    gp = min(PAGES_PER_STEP, n_pages)
    assert n_pages % gp == 0 and t_new <= SUBLANES
    assert page >= MAX_DISTANCE and rel_bias.shape[0] >= MAX_EXACT
    ck = cache_k.reshape(depth * n_pool, page * N_HEADS, HEAD_DIM)
    cv = cache_v.reshape(depth * n_pool, page * N_HEADS, HEAD_DIM)
    base = layer * n_pool
    rows = N_HEADS * 2 * SUBLANES
    cols = gp * page * N_HEADS
    nb = rel_bias.shape[0]

    n_groups = n_pages // gp

    def page_spec(j, lag):
        def index(b, g, pt):
            grp = jnp.clip(g - lag, 0, n_groups - 1)
            return (base + pt[b * n_pages + grp * gp + j], 0, 0)
        return pl.BlockSpec((None, page * N_HEADS, HEAD_DIM), index)

    seq_spec = pl.BlockSpec((None, SUBLANES, ATTN_W), lambda b, g, pt: (b, 0, 0))
    new_spec = pl.BlockSpec((None, SUBLANES * N_HEADS, HEAD_DIM), lambda b, g, pt: (b, 0, 0))
    kernel = functools.partial(_sattn_kernel, gp=gp, page=page, t_new=t_new, lam_init=lam_init,
                               num_buckets=nb)
    grid_spec = pltpu.PrefetchScalarGridSpec(
        num_scalar_prefetch=1,
        grid=(nseq, n_groups + 1),
        in_specs=[seq_spec, new_spec, new_spec,
                  pl.BlockSpec((4, D_HALF), lambda b, g, pt: (0, 0)),
                  pl.BlockSpec((1, HEAD_DIM), lambda b, g, pt: (0, 0)),
                  pl.BlockSpec(memory_space=pltpu.SMEM)]
                 + [page_spec(j, 0) for j in range(gp)] + [page_spec(j, 1) for j in range(gp)],
        out_specs=seq_spec,
        scratch_shapes=[
            pltpu.VMEM((rows, HEAD_DIM), F32),
            pltpu.VMEM((2, rows, cols), F32),
            pltpu.VMEM((rows, cols), F32),
            pltpu.VMEM((rows, 1), F32),
            pltpu.VMEM((rows, 1), F32),
            pltpu.VMEM((rows, HEAD_DIM), F32),
        ],
    )
    return pl.pallas_call(
        kernel,
        grid_spec=grid_spec,
        out_shape=jax.ShapeDtypeStruct((nseq, SUBLANES, ATTN_W), F32),
        compiler_params=_params(2),
        name="attn_sample",
    )(page_table.reshape(-1), q8, kn, vn, lam_params, g_subln, rel_bias,
      *([ck] * gp), *([cv] * gp))


def _mixer_kernel(a_ref, u_ref, gvn_ref, ga_ref, gb_ref, x_ref, ws_ref, bs_ref, wa_ref, wb_ref,
                  wo_ref, gpost_ref, o_ref, b_scr):
    tm = x_ref.shape[0]
    r = lax.broadcasted_iota(jnp.int32, (CHUNK, CHUNK), 0)
    c = lax.broadcasted_iota(jnp.int32, (CHUNK, CHUNK), 1)
    causal = r >= c
    for grp in range(N_GROUPS):
        cols = slice(grp * CHUNK, (grp + 1) * CHUNK)
        wg = ws_ref[grp]
        wg = jnp.where(causal, wg, jnp.zeros_like(wg))
        for ch in range(tm // CHUNK):
            rws = slice(ch * CHUNK, (ch + 1) * CHUNK)
            sp = jnp.dot(wg, gvn_ref[rws, cols].astype(BF16), preferred_element_type=F32)
            sp = sp + bs_ref[:, cols]
            b_scr[rws, cols] = (u_ref[rws, cols] * sp).astype(BF16)
    ya = jnp.dot(a_ref[...], wa_ref[...], preferred_element_type=F32)
    yb = jnp.dot(b_scr[...], wb_ref[...], preferred_element_type=F32)
    mix = jax.nn.sigmoid(ga_ref[...]) * ya + jax.nn.sigmoid(gb_ref[...]) * yb
    mo = jnp.dot(mix.astype(BF16), wo_ref[...], preferred_element_type=F32)
    o_ref[...] = x_ref[...] + _rms(mo, gpost_ref[...])


def _mixer(a, u, gvn, ga, gb, x, ws, bs_tile, w_a, w_b, w_out, g_post, tm):
    n, d = x.shape
    row = lambda w: pl.BlockSpec((tm, w), lambda i: (i, 0))
    return pl.pallas_call(
        _mixer_kernel,
        grid=(n // tm,),
        in_specs=[row(ATTN_W), row(GMLP_W), row(GMLP_W), row(d), row(d), row(d),
                  _resident(ws.shape), _resident(bs_tile.shape), _resident(w_a.shape),
                  _resident(w_b.shape), _resident(w_out.shape), _resident((1, d))],
        out_specs=row(d),
        out_shape=jax.ShapeDtypeStruct((n, d), F32),
        scratch_shapes=[pltpu.VMEM((tm, GMLP_W), BF16)],
        compiler_params=_params(1),
        name="mixer",
    )(a, u, gvn, ga, gb, x, ws, bs_tile, w_a, w_b, w_out, g_post)


def _ffn_kernel(x_ref, gpre_ref, w1_ref, w2_ref, gpost_ref, o_ref):
    d_ff = w2_ref.shape[0]
    x = x_ref[...]
    h = _rms(x, gpre_ref[...]).astype(BF16)
    acc = jnp.zeros(x.shape, F32)
    for c0 in range(0, d_ff, FF_TILE):
        gate = jnp.dot(h, w1_ref[:, c0:c0 + FF_TILE], preferred_element_type=F32)
        up = jnp.dot(h, w1_ref[:, d_ff + c0:d_ff + c0 + FF_TILE], preferred_element_type=F32)
        act = (gate * jax.nn.sigmoid(gate) * up).astype(BF16)
        acc = acc + jnp.dot(act, w2_ref[c0:c0 + FF_TILE, :], preferred_element_type=F32)
    o_ref[...] = x + _rms(acc, gpost_ref[...])


def _ffn(x, g_pre, w1, w2, g_post, tm):
    n, d = x.shape
    assert w2.shape[0] % FF_TILE == 0
    row = pl.BlockSpec((tm, d), lambda i: (i, 0))
    return pl.pallas_call(
        _ffn_kernel,
        grid=(n // tm,),
        in_specs=[row, _resident((1, d)), _resident(w1.shape), _resident(w2.shape), _resident((1, d))],
        out_specs=row,
        out_shape=jax.ShapeDtypeStruct((n, d), F32),
        compiler_params=_params(1),
        name="ffn",
    )(x, g_pre, w1, w2, g_post)


def kernel(x_prompt, x_sample, cache_k, cache_v, page_table, rel_bias, g_mix_pre, w_in, lam_q1, lam_k1, lam_q2, lam_k2, g_subln, g_gmlp_v, w_spatial, b_spatial, w_branch_a, w_branch_b, w_out, g_mix_post, g_ffn_pre, w_ffn_in, w_ffn_out, g_ffn_post):
    batch, seq, d = x_prompt.shape
    nseq, t_new, _ = x_sample.shape
    depth = w_in.shape[0]
    n_s = nseq * t_new
    assert n_s % CHUNK == 0 and CHUNK % t_new == 0
    attn_tile = min(ATTN_TILE, seq)
    tm_p = attn_tile
    tm_s = CHUNK

    bias_tiles = _bias_tiles(rel_bias, attn_tile)
    xp = x_prompt.reshape(batch * seq, d)
    xs = x_sample.reshape(n_s, d)
    row = lambda v: v.reshape(1, -1)
    eye = jnp.eye(CHUNK // t_new, dtype=F32)

    outs = {name: [] for name in ("gvs",)}
    kv_p = kv_s = None
    for l in range(depth):
        lam_init = 0.8 - 0.6 * math.exp(-0.3 * l)
        lam_params = jnp.stack([lam_q1[l], lam_k1[l], lam_q2[l], lam_k2[l]])
        w_in_l = w_in[l].astype(BF16)
        w_a, w_b, w_o = (w_branch_a[l].astype(BF16), w_branch_b[l].astype(BF16), w_out[l].astype(BF16))
        w1, w2 = w_ffn_in[l].astype(BF16), w_ffn_out[l].astype(BF16)
        gsub = row(g_subln[l])
        ws_p = w_spatial[l].astype(BF16)
        bs_p = jnp.repeat(b_spatial[l].T, CHUNK, axis=1)
        ws_s = jax.vmap(lambda w: jnp.kron(eye, w[:t_new, :t_new]))(w_spatial[l]).astype(BF16)
        bs_s = jnp.repeat(jnp.tile(b_spatial[l][:, :t_new], (1, CHUNK // t_new)).T, CHUNK, axis=1)

        qk, vt, kp, vp, u, gvn, ga, gb = _proj(xp, row(g_mix_pre[l]), w_in_l, row(g_gmlp_v[l]), tm_p, kv_p)
        kv_p = (kp, vp)
        a = _attn_prompt(qk, vt, bias_tiles, rel_bias, lam_params, gsub, batch, seq, lam_init)
        xp = _mixer(a, u, gvn, ga, gb, xp, ws_p, bs_p, w_a, w_b, w_o, row(g_mix_post[l]), tm_p)
        xp = _ffn(xp, row(g_ffn_pre[l]), w1, w2, row(g_ffn_post[l]), tm_p)

        qk, _, ks, vs, u, gvn, ga, gb = _proj(xs, row(g_mix_pre[l]), w_in_l, row(g_gmlp_v[l]), tm_s, kv_s)
        kv_s = (ks, vs)
        pad_tokens = lambda t: jnp.pad(t, ((0, 0), (0, SUBLANES - t_new)) + ((0, 0),) * (t.ndim - 2))
        q8 = pad_tokens(qk[:, :ATTN_W].astype(F32).reshape(nseq, t_new, ATTN_W))
        kn = pad_tokens(ks[l].reshape(nseq, t_new, ATTN_W)).reshape(nseq, SUBLANES * N_HEADS, HEAD_DIM)
        vn = pad_tokens(vs[l].reshape(nseq, t_new, ATTN_W)).reshape(nseq, SUBLANES * N_HEADS, HEAD_DIM)
        a8 = _attn_sample(q8, kn, vn, cache_k, cache_v, page_table, l, rel_bias, lam_params, gsub,
                          t_new, lam_init)
        a = a8[:, :t_new].reshape(n_s, ATTN_W).astype(BF16)
        xs = _mixer(a, u, gvn, ga, gb, xs, ws_s, bs_s, w_a, w_b, w_o, row(g_mix_post[l]), tm_s)
        xs = _ffn(xs, row(g_ffn_pre[l]), w1, w2, row(g_ffn_post[l]), tm_s)
        outs["gvs"].append(gvn.reshape(nseq, t_new, GMLP_W))

    return (xp.reshape(batch, seq, d), xs.reshape(nseq, t_new, d),
            kv_p[0].reshape(depth, batch, seq, N_HEADS, HEAD_DIM),
            kv_p[1].reshape(depth, batch, seq, N_HEADS, HEAD_DIM),
            kv_s[0].reshape(depth, nseq, t_new, N_HEADS, HEAD_DIM),
            kv_s[1].reshape(depth, nseq, t_new, N_HEADS, HEAD_DIM), jnp.stack(outs["gvs"]))
```
